```python
import jax, jax.numpy as jnp
from jax import lax
import numpy as np

D_MODEL = 4096
BATCH = 8
SEQ = 2048
DEPTH = 4
DEC_BATCH = 2
DEC_SEQ = 8192
PAST_LEN = 128

N_HEADS = 32
QK_NOPE = 128
QK_ROPE = 64
QK_HEAD = QK_NOPE + QK_ROPE
V_HEAD = 128
Q_LORA = 1024
KV_LORA = 512
ROPE_THETA = 10000.0
ATTN_SCALE = QK_HEAD ** -0.5
Q_BLOCK = 128
D_CONV = D_MODEL // 2
CONV_WIDTH = 31
CONV_PAD = (CONV_WIDTH - 1) // 2
D_FF = -(-(8 * D_MODEL) // (3 * 256)) * 256
EPS = 1e-6

OFF_CQ = 2 * D_CONV
OFF_CKV = OFF_CQ + Q_LORA
OFF_KR = OFF_CKV + KV_LORA
OFF_GATE = OFF_KR + QK_ROPE
D_IN = OFF_GATE + 2 * D_MODEL

kernel_name = "hybrid_conv_mla_gated_encoder"


def rms_norm(x, g):
    xf = x.astype(jnp.float32)
    y = xf * lax.rsqrt(jnp.mean(xf * xf, axis=-1, keepdims=True) + EPS)
    return (y * g.astype(jnp.float32)).astype(x.dtype)


def layer_norm(x, g, b):
    xf = x.astype(jnp.float32)
    mu = jnp.mean(xf, axis=-1, keepdims=True)
    var = jnp.mean(jnp.square(xf - mu), axis=-1, keepdims=True)
    y = (xf - mu) * lax.rsqrt(var + EPS)
    return (y * g.astype(jnp.float32) + b.astype(jnp.float32)).astype(x.dtype)


def rotary_tables(seq_len):
    inv_freq = ROPE_THETA ** (-jnp.arange(0, QK_ROPE, 2, dtype=jnp.float32) / QK_ROPE)
    ang = jnp.arange(seq_len, dtype=jnp.float32)[:, None] * inv_freq[None, :]
    return jnp.cos(ang), jnp.sin(ang)


def apply_rope(x, cos, sin):
    cos = cos.astype(x.dtype)
    sin = sin.astype(x.dtype)
    x1, x2 = jnp.split(x, 2, axis=-1)
    return jnp.concatenate([x1 * cos - x2 * sin, x2 * cos + x1 * sin], axis=-1)


def conv_branch(u, conv_w, conv_b, ln_g, ln_b, w_conv_out):
    a, g = jnp.split(u, 2, axis=-1)
    h = a * jax.nn.sigmoid(g)
    h = lax.conv_general_dilated(
        h, conv_w[:, None, :].astype(h.dtype), window_strides=(1,),
        padding=[(CONV_PAD, CONV_PAD)], dimension_numbers=("NWC", "WIO", "NWC"),
        feature_group_count=D_CONV) + conv_b
    h = jax.nn.silu(layer_norm(h, ln_g, ln_b))
    return h @ w_conv_out


def mla_branch(c_q, c_kv, k_r, q_norm, w_uq, kv_norm, w_ukv, w_o_attn, cos, sin):
    B, S = c_q.shape[0], c_q.shape[1]
    q = (rms_norm(c_q, q_norm) @ w_uq).reshape(B, S, N_HEADS, QK_HEAD)
    q_nope = q[..., :QK_NOPE]
    q_rope = apply_rope(q[..., QK_NOPE:], cos[:, None, :], sin[:, None, :])
    kv = (rms_norm(c_kv, kv_norm) @ w_ukv).reshape(B, S, N_HEADS, QK_NOPE + V_HEAD)
    k_nope = kv[..., :QK_NOPE]
    v = kv[..., QK_NOPE:]
    k_rope = apply_rope(k_r, cos, sin)
    nb = S // Q_BLOCK
    qn_blocks = q_nope.reshape(B, nb, Q_BLOCK, N_HEADS, QK_NOPE).transpose(1, 0, 2, 3, 4)
    qr_blocks = q_rope.reshape(B, nb, Q_BLOCK, N_HEADS, QK_ROPE).transpose(1, 0, 2, 3, 4)

    def attend(blk):
        qn, qr = blk
        s = (jnp.einsum("bqhd,bkhd->bhqk", qn, k_nope)
             + jnp.einsum("bqhr,bkr->bhqk", qr, k_rope))
        p = jax.nn.softmax(s.astype(jnp.float32) * ATTN_SCALE, axis=-1).astype(v.dtype)
        return jnp.einsum("bhqk,bkhd->bqhd", p, v)

    o = lax.map(attend, (qn_blocks, qr_blocks))
    o = o.transpose(1, 0, 2, 3, 4).reshape(B, S, N_HEADS * V_HEAD)
    return o @ w_o_attn


def encoder_layer(x, cos, sin, pre_norm_mix, w_in, conv_w, conv_b, conv_ln_g, conv_ln_b,
                  w_conv_out, q_norm, w_uq, kv_norm, w_ukv, w_o_attn, w_out, post_norm_mix,
                  pre_norm_ffn, w_gate, w_up, w_down, post_norm_ffn):
    h = rms_norm(x, pre_norm_mix)
    u = h @ w_in
    y_conv = conv_branch(u[..., :OFF_CQ], conv_w, conv_b, conv_ln_g, conv_ln_b, w_conv_out)
    y_attn = mla_branch(u[..., OFF_CQ:OFF_CKV], u[..., OFF_CKV:OFF_KR], u[..., OFF_KR:OFF_GATE],
                        q_norm, w_uq, kv_norm, w_ukv, w_o_attn, cos, sin)
    g_conv = jax.nn.sigmoid(u[..., OFF_GATE:OFF_GATE + D_MODEL])
    g_attn = jax.nn.sigmoid(u[..., OFF_GATE + D_MODEL:])
    mixed = (g_conv * y_conv + g_attn * y_attn) @ w_out
    x = x + rms_norm(mixed, post_norm_mix)
    h = rms_norm(x, pre_norm_ffn)
    f = (jax.nn.silu(h @ w_gate) * (h @ w_up)) @ w_down
    return x + rms_norm(f, post_norm_ffn)


def trunk(x, weights):
    cos, sin = rotary_tables(x.shape[1])
    for l in range(DEPTH):
        x = encoder_layer(x, cos, sin, *[w[l] for w in weights])
    return x


def setup_inputs(seed: int = 0) -> dict:
    key = jax.random.key(seed)
    ks = jax.random.split(key, 24)
    f32 = jnp.float32

    def nrm(k, shape, fan_in):
        return jax.random.normal(k, shape, f32) * (fan_in ** -0.5)

    def gain(k, n):
        return 1.0 + 0.02 * jax.random.normal(k, (DEPTH, n), f32)

    def bias(k, n):
        return 0.02 * jax.random.normal(k, (DEPTH, n), f32)

    return {
        "x_prompt": jax.random.normal(ks[0], (BATCH, SEQ, D_MODEL), f32),
        "x_sample": jax.random.normal(ks[1], (DEC_BATCH, DEC_SEQ, D_MODEL), f32),
        "pre_norm_mix": gain(ks[2], D_MODEL),
        "w_in": nrm(ks[3], (DEPTH, D_MODEL, D_IN), D_MODEL),
        "conv_w": nrm(ks[4], (DEPTH, CONV_WIDTH, D_CONV), CONV_WIDTH),
        "conv_b": bias(ks[5], D_CONV),
        "conv_ln_g": gain(ks[6], D_CONV),
        "conv_ln_b": bias(ks[7], D_CONV),
        "w_conv_out": nrm(ks[8], (DEPTH, D_CONV, D_MODEL), D_CONV),
        "q_norm": gain(ks[9], Q_LORA),
        "w_uq": nrm(ks[10], (DEPTH, Q_LORA, N_HEADS * QK_HEAD), Q_LORA),
        "kv_norm": gain(ks[11], KV_LORA),
        "w_ukv": nrm(ks[12], (DEPTH, KV_LORA, N_HEADS * (QK_NOPE + V_HEAD)), KV_LORA),
        "w_o_attn": nrm(ks[13], (DEPTH, N_HEADS * V_HEAD, D_MODEL), N_HEADS * V_HEAD),
        "w_out": nrm(ks[14], (DEPTH, D_MODEL, D_MODEL), D_MODEL),
        "post_norm_mix": gain(ks[15], D_MODEL),
        "pre_norm_ffn": gain(ks[16], D_MODEL),
        "w_gate": nrm(ks[17], (DEPTH, D_MODEL, D_FF), D_MODEL),
        "w_up": nrm(ks[18], (DEPTH, D_MODEL, D_FF), D_MODEL),
        "w_down": nrm(ks[19], (DEPTH, D_FF, D_MODEL), D_FF),
        "post_norm_ffn": gain(ks[20], D_MODEL),
    }


def reference(x_prompt, x_sample, pre_norm_mix, w_in, conv_w, conv_b, conv_ln_g, conv_ln_b,
              w_conv_out, q_norm, w_uq, kv_norm, w_ukv, w_o_attn, w_out, post_norm_mix,
              pre_norm_ffn, w_gate, w_up, w_down, post_norm_ffn):
    weights = (pre_norm_mix, w_in, conv_w, conv_b, conv_ln_g, conv_ln_b, w_conv_out,
               q_norm, w_uq, kv_norm, w_ukv, w_o_attn, w_out, post_norm_mix,
               pre_norm_ffn, w_gate, w_up, w_down, post_norm_ffn)
    y_prompt = trunk(x_prompt, weights)
    y_sample = trunk(x_sample, weights)
    return (y_prompt, y_sample)
```

```python
import functools

import jax
import jax.numpy as jnp
from jax import lax
from jax.experimental import pallas as pl
from jax.experimental.pallas import tpu as pltpu

F32 = jnp.float32
BF16 = jnp.bfloat16

QK_NOPE = 128
QK_ROPE = 64
QK_HEAD = QK_NOPE + QK_ROPE
V_HEAD = 128
ROPE_THETA = 10000.0
ATTN_SCALE = QK_HEAD ** -0.5
EPS = 1e-6

LANES = 128
HALF_ROPE = QK_ROPE // 2
QK_PAD = QK_NOPE + LANES
BF16_ROWS = 16
VMEM_LIMIT = 56 * 1024 * 1024


def _tile(dim, pref, align):
    if dim <= pref:
        return dim
    t = (pref // align) * align
    while t >= align:
        if dim % t == 0:
            return t
        t -= align
    return dim


def _params(sem):
    return pltpu.CompilerParams(dimension_semantics=sem, vmem_limit_bytes=VMEM_LIMIT)


def _rms(x, g):
    return x * lax.rsqrt(jnp.mean(x * x, axis=-1, keepdims=True) + EPS) * g


def _rope(x, cos, sg):
    return x * cos + pltpu.roll(x, 2 * HALF_ROPE, 1) * sg


def _mm_kernel(x_ref, w_ref, o_ref, *scratch, nk):
    part = jnp.dot(x_ref[...], w_ref[...], preferred_element_type=F32)
    if nk == 1:
        o_ref[...] = part.astype(o_ref.dtype)
        return
    acc_ref, = scratch
    k = pl.program_id(2)

    @pl.when(k == 0)
    def _():
        acc_ref[...] = part

    @pl.when(k > 0)
    def _():
        acc_ref[...] += part

    @pl.when(k == nk - 1)
    def _():
        o_ref[...] = acc_ref[...].astype(o_ref.dtype)


def _matmul(x, w, out_dtype, *, bm, bn, bk, name):
    m, kdim = x.shape
    n = w.shape[1]
    bm, bn, bk = _tile(m, bm, 8), _tile(n, bn, LANES), _tile(kdim, bk, LANES)
    nk = kdim // bk
    scratch = [pltpu.VMEM((bm, bn), F32)] if nk > 1 else []
    return pl.pallas_call(
        functools.partial(_mm_kernel, nk=nk),
        out_shape=jax.ShapeDtypeStruct((m, n), out_dtype),
        grid=(m // bm, n // bn, nk),
        in_specs=[pl.BlockSpec((bm, bk), lambda i, j, k: (i, k)),
                  pl.BlockSpec((bk, bn), lambda i, j, k: (k, j))],
        out_specs=pl.BlockSpec((bm, bn), lambda i, j, k: (i, j)),
        scratch_shapes=scratch,
        compiler_params=_params(("parallel", "parallel", "arbitrary")),
        name=name,
    )(x, w)


def _rmsnorm_kernel(x_ref, g_ref, h_ref):
    h_ref[...] = _rms(x_ref[...], g_ref[...]).astype(h_ref.dtype)


def _rmsnorm(x, g, *, br=256):
    t, d = x.shape
    br = _tile(t, br, 8)
    return pl.pallas_call(
        _rmsnorm_kernel,
        out_shape=jax.ShapeDtypeStruct((t, d), BF16),
        grid=(t // br,),
        in_specs=[pl.BlockSpec((br, d), lambda i: (i, 0)),
                  pl.BlockSpec((1, d), lambda i: (0, 0))],
        out_specs=pl.BlockSpec((br, d), lambda i: (i, 0)),
        compiler_params=_params(("parallel",)),
        name="rmsnorm",
    )(x, g.reshape(1, d))


def _resnorm_kernel(x_ref, d_ref, gpost_ref, gnext_ref, xo_ref, h_ref):
    xn = x_ref[...] + _rms(d_ref[...].astype(F32), gpost_ref[...])
    xo_ref[...] = xn
    h_ref[...] = _rms(xn, gnext_ref[...]).astype(h_ref.dtype)


def _resnorm(x, delta, g_post, g_next, *, br=128):
    t, d = x.shape
    br = _tile(t, br, 8)
    row = pl.BlockSpec((br, d), lambda i: (i, 0))
    vec = pl.BlockSpec((1, d), lambda i: (0, 0))
    return pl.pallas_call(
        _resnorm_kernel,
        out_shape=(jax.ShapeDtypeStruct((t, d), F32), jax.ShapeDtypeStruct((t, d), BF16)),
        grid=(t // br,),
        in_specs=[row, row, vec, vec],
        out_specs=(row, row),
        compiler_params=_params(("parallel",)),
        name="resnorm",
    )(x, delta, g_post.reshape(1, d), g_next.reshape(1, d))


def _qproj_kernel(c_ref, g_ref, w_ref, cos_ref, sg_ref, o_ref, xn_ref, *, heads):
    @pl.when(pl.program_id(1) == 0)
    def _():
        xn_ref[...] = _rms(c_ref[...].astype(F32), g_ref[...]).astype(BF16)

    acc = jnp.dot(xn_ref[...], w_ref[...], preferred_element_type=F32) * ATTN_SCALE
    cos, sg = cos_ref[...], sg_ref[...]
    for h in range(heads):
        lo = h * QK_PAD
        o_ref[:, lo:lo + QK_NOPE] = acc[:, lo:lo + QK_NOPE].astype(o_ref.dtype)
        o_ref[:, lo + QK_NOPE:lo + QK_PAD] = _rope(
            acc[:, lo + QK_NOPE:lo + QK_PAD], cos, sg).astype(o_ref.dtype)


def _qproj(u_lat, g, w, cos4, sg4, *, seq, q_lora, bm=1024, heads=4):
    t = u_lat.shape[0]
    n = w.shape[1]
    bm = _tile(seq, bm, BF16_ROWS)
    heads = min(heads, n // QK_PAD)
    bn = heads * QK_PAD
    spb = seq // bm
    tab = pl.BlockSpec((bm, LANES), lambda i, j: (i % spb, 0))
    return pl.pallas_call(
        functools.partial(_qproj_kernel, heads=heads),
        out_shape=jax.ShapeDtypeStruct((t, n), BF16),
        grid=(t // bm, n // bn),
        in_specs=[pl.BlockSpec((bm, q_lora), lambda i, j: (i, 0)),
                  pl.BlockSpec((1, q_lora), lambda i, j: (0, 0)),
                  pl.BlockSpec((q_lora, bn), lambda i, j: (0, j)),
                  tab, tab],
        out_specs=pl.BlockSpec((bm, bn), lambda i, j: (i, j)),
        scratch_shapes=[pltpu.VMEM((bm, q_lora), BF16)],
        compiler_params=_params(("parallel", "arbitrary")),
        name="qproj",
    )(u_lat, g.reshape(1, q_lora), w, cos4, sg4)


def _kvproj_kernel(c_ref, kr_ref, g_ref, wk_ref, wv_ref, cos_ref, sg_ref, k_ref, v_ref,
                   xn_ref, krr_ref, *, heads):
    @pl.when(pl.program_id(1) == 0)
    def _():
        xn_ref[...] = _rms(c_ref[...].astype(F32), g_ref[...]).astype(BF16)
        krr_ref[...] = _rope(kr_ref[...].astype(F32), cos_ref[...], sg_ref[...]).astype(BF16)

    xn = xn_ref[...]
    kn = jnp.dot(xn, wk_ref[...], preferred_element_type=F32)
    v_ref[...] = jnp.dot(xn, wv_ref[...], preferred_element_type=F32).astype(v_ref.dtype)
    for h in range(heads):
        lo = h * QK_PAD
        k_ref[:, lo:lo + QK_NOPE] = kn[:, h * QK_NOPE:(h + 1) * QK_NOPE].astype(k_ref.dtype)
        k_ref[:, lo + QK_NOPE:lo + QK_PAD] = krr_ref[...]


def _kvproj(u_lat, g, wk, wv, cos4, sg4, *, seq, q_lora, kv_lora, bm=1024, heads=4):
    t = u_lat.shape[0]
    n_heads = wk.shape[1] // QK_NOPE
    bm = _tile(seq, bm, BF16_ROWS)
    heads = min(heads, n_heads)
    spb = seq // bm
    assert q_lora % kv_lora == 0 and (q_lora + kv_lora) % LANES == 0
    ckv_blk = q_lora // kv_lora
    kr_blk = (q_lora + kv_lora) // LANES
    tab = pl.BlockSpec((bm, LANES), lambda i, j: (i % spb, 0))
    return pl.pallas_call(
        functools.partial(_kvproj_kernel, heads=heads),
        out_shape=(jax.ShapeDtypeStruct((t, n_heads * QK_PAD), BF16),
                   jax.ShapeDtypeStruct((t, n_heads * V_HEAD), BF16)),
        grid=(t // bm, n_heads // heads),
        in_specs=[pl.BlockSpec((bm, kv_lora), lambda i, j: (i, ckv_blk)),
                  pl.BlockSpec((bm, LANES), lambda i, j: (i, kr_blk)),
                  pl.BlockSpec((1, kv_lora), lambda i, j: (0, 0)),
                  pl.BlockSpec((kv_lora, heads * QK_NOPE), lambda i, j: (0, j)),
                  pl.BlockSpec((kv_lora, heads * V_HEAD), lambda i, j: (0, j)),
                  tab, tab],
        out_specs=(pl.BlockSpec((bm, heads * QK_PAD), lambda i, j: (i, j)),
                   pl.BlockSpec((bm, heads * V_HEAD), lambda i, j: (i, j))),
        scratch_shapes=[pltpu.VMEM((bm, kv_lora), BF16), pltpu.VMEM((bm, LANES), BF16)],
        compiler_params=_params(("parallel", "arbitrary")),
        name="kvproj",
    )(u_lat, u_lat, g.reshape(1, kv_lora), wk, wv, cos4, sg4)


def _attn_kernel(q_ref, k_ref, v_ref, o_ref, *, tk, nk):
    q = q_ref[...]
    tq = q.shape[0]

    def body(j, carry):
        m, l, acc = carry
        rows = pl.ds(pl.multiple_of(j * tk, tk), tk)
        s = lax.dot_general(q, k_ref[rows, :], (((1,), (1,)), ((), ())),
                            preferred_element_type=F32)
        m_new = jnp.maximum(m, jnp.max(s, axis=-1, keepdims=True))
        alpha = jnp.exp(m - m_new)
        p = jnp.exp(s - m_new)
        l = alpha * l + jnp.sum(p, axis=-1, keepdims=True)
        acc = alpha * acc + jnp.dot(p.astype(BF16), v_ref[rows, :], preferred_element_type=F32)
        return m_new, l, acc

    m0 = jnp.full((tq, 1), -jnp.inf, F32)
    l0 = jnp.zeros((tq, 1), F32)
    acc0 = jnp.zeros((tq, V_HEAD), F32)
    _, l, acc = lax.fori_loop(0, nk, body, (m0, l0, acc0))
    o_ref[...] = (acc / l).astype(o_ref.dtype)


def _attention(q, k, v, *, seq, tq=512, tk=512):
    t = q.shape[0]
    n_heads = q.shape[1] // QK_PAD
    tq, tk = _tile(seq, tq, BF16_ROWS), _tile(seq, tk, BF16_ROWS)
    qpb = seq // tq
    return pl.pallas_call(
        functools.partial(_attn_kernel, tk=tk, nk=seq // tk),
        out_shape=jax.ShapeDtypeStruct((t, n_heads * V_HEAD), BF16),
        grid=(t // seq, n_heads, qpb),
        in_specs=[pl.BlockSpec((tq, QK_PAD), lambda b, h, i: (b * qpb + i, h)),
                  pl.BlockSpec((seq, QK_PAD), lambda b, h, i: (b, h)),
                  pl.BlockSpec((seq, V_HEAD), lambda b, h, i: (b, h))],
        out_specs=pl.BlockSpec((tq, V_HEAD), lambda b, h, i: (b * qpb + i, h)),
        compiler_params=_params(("parallel", "parallel", "arbitrary")),
        name="attention",
    )(q, k, v)


def _conv_kernel(a_ref, g_ref, ap_ref, gp_ref, an_ref, gn_ref, w_ref, b_ref, lg_ref, lb_ref,
                 o_ref, h_ref, c_ref, *, ts, tiles_per_seq, width, rows):
    i = pl.program_id(0)
    halo = BF16_ROWS

    def glu(a, g):
        return a[...].astype(F32) * jax.nn.sigmoid(g[...].astype(F32))

    h_ref[halo:halo + ts, :] = glu(a_ref, g_ref)
    first = (i % tiles_per_seq) == 0
    last = (i % tiles_per_seq) == tiles_per_seq - 1
    h_ref[0:halo, :] = jnp.where(first, 0.0, glu(ap_ref, gp_ref))
    h_ref[halo + ts:2 * halo + ts, :] = jnp.where(last, 0.0, glu(an_ref, gn_ref))

    pad = (width - 1) // 2
    base = halo - pad
    span = ((base + width - 1) // 8) * 8
    n_chunks = a_ref.shape[1] // LANES

    def lane_chunk(c, carry):
        lanes = pl.ds(pl.multiple_of(c * LANES, LANES), LANES)
        wts = [w_ref[pl.ds(k, 1), lanes] for k in range(width)]
        bias = b_ref[:, lanes]
        for r0 in range(0, ts, rows):
            acc = jnp.zeros((rows, LANES), F32)
            for r in range(8):
                taps = [k for k in range(width) if (base + k) % 8 == r]
                if not taps:
                    continue
                win = h_ref[pl.ds(r0 + r, rows + span), lanes]
                for k in taps:
                    off = (base + k) - r
                    acc = acc + wts[k] * win[off:off + rows, :]
            c_ref[pl.ds(r0, rows), lanes] = acc + bias
        return carry

    lax.fori_loop(0, n_chunks, lane_chunk, 0)

    y = c_ref[...]
    mu = jnp.mean(y, axis=-1, keepdims=True)
    yc = y - mu
    var = jnp.mean(yc * yc, axis=-1, keepdims=True)
    z = yc * lax.rsqrt(var + EPS) * lg_ref[...] + lb_ref[...]
    o_ref[...] = (z * jax.nn.sigmoid(z)).astype(o_ref.dtype)


def _conv_branch(u_main, conv_w, conv_b, ln_g, ln_b, *, seq, d_conv, ts=256, rows=64):
    t = u_main.shape[0]
    width = conv_w.shape[0]
    halo = BF16_ROWS
    assert (width - 1) // 2 <= halo
    ts = _tile(seq, ts, halo)
    rows = _tile(ts, rows, 8)
    hb = ts // halo
    last_blk = t // halo - 1
    cur = lambda c: pl.BlockSpec((ts, d_conv), lambda i: (i, c))
    prev = lambda c: pl.BlockSpec((halo, d_conv), lambda i: (jnp.maximum(i * hb - 1, 0), c))
    nxt = lambda c: pl.BlockSpec((halo, d_conv), lambda i: (jnp.minimum((i + 1) * hb, last_blk), c))
    vec = pl.BlockSpec((1, d_conv), lambda i: (0, 0))
    return pl.pallas_call(
        functools.partial(_conv_kernel, ts=ts, tiles_per_seq=seq // ts, width=width, rows=rows),
        out_shape=jax.ShapeDtypeStruct((t, d_conv), BF16),
        grid=(t // ts,),
        in_specs=[cur(0), cur(1), prev(0), prev(1), nxt(0), nxt(1),
                  pl.BlockSpec((width, d_conv), lambda i: (0, 0)), vec, vec, vec],
        out_specs=pl.BlockSpec((ts, d_conv), lambda i: (i, 0)),
        scratch_shapes=[pltpu.VMEM((ts + 2 * halo, d_conv), F32),
                        pltpu.VMEM((ts, d_conv), F32)],
        compiler_params=_params(("parallel",)),
        name="conv_branch",
    )(u_main, u_main, u_main, u_main, u_main, u_main, conv_w,
      conv_b.reshape(1, d_conv), ln_g.reshape(1, d_conv), ln_b.reshape(1, d_conv))


def _merge_kernel(hc_ref, o_ref, wc_ref, wo_ref, gc_ref, ga_ref, out_ref):
    yc = jnp.dot(hc_ref[...], wc_ref[...], preferred_element_type=F32)
    ya = jnp.dot(o_ref[...], wo_ref[...], preferred_element_type=F32)
    gc = jax.nn.sigmoid(gc_ref[...].astype(F32))
    ga = jax.nn.sigmoid(ga_ref[...].astype(F32))
    out_ref[...] = (gc * yc + ga * ya).astype(out_ref.dtype)


def _merge(hc, o, wc, wo, u_main, *, d_conv, bm=1024, bn=512):
    t = hc.shape[0]
    d = wc.shape[1]
    bm, bn = _tile(t, bm, BF16_ROWS), _tile(d, bn, LANES)
    assert (2 * d_conv) % bn == 0
    g0 = 2 * d_conv // bn
    nb = d // bn
    return pl.pallas_call(
        _merge_kernel,
        out_shape=jax.ShapeDtypeStruct((t, d), BF16),
        grid=(t // bm, nb),
        in_specs=[pl.BlockSpec((bm, hc.shape[1]), lambda i, j: (i, 0)),
                  pl.BlockSpec((bm, o.shape[1]), lambda i, j: (i, 0)),
                  pl.BlockSpec((wc.shape[0], bn), lambda i, j: (0, j)),
                  pl.BlockSpec((wo.shape[0], bn), lambda i, j: (0, j)),
                  pl.BlockSpec((bm, bn), lambda i, j: (i, g0 + j)),
                  pl.BlockSpec((bm, bn), lambda i, j: (i, g0 + nb + j))],
        out_specs=pl.BlockSpec((bm, bn), lambda i, j: (i, j)),
        compiler_params=_params(("parallel", "parallel")),
        name="merge",
    )(hc, o, wc, wo, u_main, u_main)


def _gateup_kernel(h_ref, w_ref, o_ref, *, bn):
    acc = jnp.dot(h_ref[...], w_ref[...], preferred_element_type=F32)
    gate, up = acc[:, :bn], acc[:, bn:]
    o_ref[...] = (gate * jax.nn.sigmoid(gate) * up).astype(o_ref.dtype)


def _gateup(h, w_gu, *, bn, bm=1024):
    t, d = h.shape
    f = w_gu.shape[1] // 2
    bm = _tile(t, bm, BF16_ROWS)
    return pl.pallas_call(
        functools.partial(_gateup_kernel, bn=bn),
        out_shape=jax.ShapeDtypeStruct((t, f), BF16),
        grid=(t // bm, f // bn),
        in_specs=[pl.BlockSpec((bm, d), lambda i, j: (i, 0)),
                  pl.BlockSpec((d, 2 * bn), lambda i, j: (0, j))],
        out_specs=pl.BlockSpec((bm, bn), lambda i, j: (i, j)),
        compiler_params=_params(("parallel", "parallel")),
        name="gateup",
    )(h, w_gu)


FF_TILE = 512
FF_ALIGN = 1024


def _prep_weights(w_in, w_conv_out, w_uq, w_ukv, w_o_attn, w_out, w_gate, w_up, w_down,
                  *, d_conv, q_lora, kv_lora):
    depth, d, _ = w_in.shape
    off_cq = 2 * d_conv
    off_ckv = off_cq + q_lora
    off_kr = off_ckv + kv_lora
    off_gate = off_kr + QK_ROPE
    zr = jnp.zeros((depth, d, HALF_ROPE), w_in.dtype)
    w_main = jnp.concatenate([w_in[..., :off_cq], w_in[..., off_gate:]], axis=-1).astype(BF16)
    w_lat = jnp.concatenate([w_in[..., off_cq:off_kr],
                             w_in[..., off_kr:off_kr + HALF_ROPE], zr,
                             w_in[..., off_kr + HALF_ROPE:off_gate], zr], axis=-1).astype(BF16)

    n_heads = w_uq.shape[-1] // QK_HEAD
    wq = w_uq.reshape(depth, q_lora, n_heads, QK_HEAD)
    zq = jnp.zeros((depth, q_lora, n_heads, HALF_ROPE), w_uq.dtype)
    wq = jnp.concatenate([wq[..., :QK_NOPE + HALF_ROPE], zq, wq[..., QK_NOPE + HALF_ROPE:], zq],
                         axis=-1).reshape(depth, q_lora, n_heads * QK_PAD).astype(BF16)

    wkv = w_ukv.reshape(depth, kv_lora, n_heads, QK_NOPE + V_HEAD)
    wk = wkv[..., :QK_NOPE].reshape(depth, kv_lora, n_heads * QK_NOPE).astype(BF16)
    wv = wkv[..., QK_NOPE:].reshape(depth, kv_lora, n_heads * V_HEAD).astype(BF16)

    d_ff = w_gate.shape[-1]
    fp = -(-d_ff // FF_ALIGN) * FF_ALIGN if d_ff > FF_ALIGN else d_ff
    bn = _tile(fp, FF_TILE, LANES)
    padc = lambda w: jnp.pad(w, ((0, 0), (0, 0), (0, fp - d_ff)))
    wg = padc(w_gate).reshape(depth, d, fp // bn, 1, bn)
    wu = padc(w_up).reshape(depth, d, fp // bn, 1, bn)
    w_gu = jnp.concatenate([wg, wu], axis=3).reshape(depth, d, 2 * fp).astype(BF16)
    w_dn = jnp.pad(w_down, ((0, 0), (0, fp - d_ff), (0, 0))).astype(BF16)

    return dict(w_main=w_main, w_lat=w_lat, wq=wq, wk=wk, wv=wv,
                wc=w_conv_out.astype(BF16), wo=w_o_attn.astype(BF16), w_out=w_out.astype(BF16),
                w_gu=w_gu, w_dn=w_dn, ff_tile=bn)


def _rope_tables(seq):
    inv_freq = ROPE_THETA ** (-jnp.arange(0, QK_ROPE, 2, dtype=F32) / QK_ROPE)
    ang = jnp.arange(seq, dtype=F32)[:, None] * inv_freq[None, :]
    cos, sin = jnp.cos(ang), jnp.sin(ang)
    return (jnp.concatenate([cos, cos, cos, cos], axis=-1),
            jnp.concatenate([-sin, -sin, sin, sin], axis=-1))


def _trunk(x3, pw, small, dims):
    b, seq, d = x3.shape
    x = x3.reshape(b * seq, d)
    depth = pw["w_main"].shape[0]
    d_conv, q_lora, kv_lora = dims
    cos4, sg4 = _rope_tables(seq)
    h = _rmsnorm(x, small["pre_norm_mix"][0])
    for l in range(depth):
        u_main = _matmul(h, pw["w_main"][l], BF16, bm=1024, bn=1024, bk=d, name="w_in_main")
        u_lat = _matmul(h, pw["w_lat"][l], BF16, bm=512, bn=pw["w_lat"].shape[-1], bk=d,
                        name="w_in_latent")
        hc = _conv_branch(u_main, small["conv_w"][l], small["conv_b"][l], small["conv_ln_g"][l],
                          small["conv_ln_b"][l], seq=seq, d_conv=d_conv)
        q = _qproj(u_lat, small["q_norm"][l], pw["wq"][l], cos4, sg4, seq=seq, q_lora=q_lora)
        k, v = _kvproj(u_lat, small["kv_norm"][l], pw["wk"][l], pw["wv"][l], cos4, sg4,
                       seq=seq, q_lora=q_lora, kv_lora=kv_lora)
        o = _attention(q, k, v, seq=seq)
        mix = _merge(hc, o, pw["wc"][l], pw["wo"][l], u_main, d_conv=d_conv)
        mixed = _matmul(mix, pw["w_out"][l], F32, bm=1024, bn=1024, bk=d, name="w_out")
        x, h = _resnorm(x, mixed, small["post_norm_mix"][l], small["pre_norm_ffn"][l])
        act = _gateup(h, pw["w_gu"][l], bn=pw["ff_tile"])
        f = _matmul(act, pw["w_dn"][l], F32, bm=1024, bn=1024, bk=1024, name="w_down")
        g_next = small["pre_norm_mix"][(l + 1) % depth]
        x, h = _resnorm(x, f, small["post_norm_ffn"][l], g_next)
    return x.reshape(b, seq, d)


def kernel(x_prompt, x_sample, pre_norm_mix, w_in, conv_w, conv_b, conv_ln_g, conv_ln_b, w_conv_out, q_norm, w_uq, kv_norm, w_ukv, w_o_attn, w_out, post_norm_mix, pre_norm_ffn, w_gate, w_up, w_down, post_norm_ffn):
    d_conv = conv_w.shape[-1]
    q_lora = q_norm.shape[-1]
    kv_lora = kv_norm.shape[-1]
    pw = _prep_weights(w_in, w_conv_out, w_uq, w_ukv, w_o_attn, w_out, w_gate, w_up, w_down,
                       d_conv=d_conv, q_lora=q_lora, kv_lora=kv_lora)
    small = dict(pre_norm_mix=pre_norm_mix, conv_w=conv_w, conv_b=conv_b, conv_ln_g=conv_ln_g,
                 conv_ln_b=conv_ln_b, q_norm=q_norm, kv_norm=kv_norm, post_norm_mix=post_norm_mix,
                 pre_norm_ffn=pre_norm_ffn, post_norm_ffn=post_norm_ffn)
    dims = (d_conv, q_lora, kv_lora)
    return (_trunk(x_prompt, pw, small, dims), _trunk(x_sample, pw, small, dims))
```

```python
import functools

import numpy as np
import jax
import jax.numpy as jnp
from jax import lax
from jax.experimental import pallas as pl
from jax.experimental.pallas import tpu as pltpu

F32 = jnp.float32
BF16 = jnp.bfloat16

QK_NOPE = 128
QK_ROPE = 64
QK_HEAD = QK_NOPE + QK_ROPE
V_HEAD = 128
ROPE_THETA = 10000.0
ATTN_SCALE = QK_HEAD ** -0.5
LOG2E = 1.4426950408889634
EPS = 1e-6

LANES = 128
HALF_ROPE = QK_ROPE // 2
QK_PAD = QK_NOPE + LANES
BF16_ROWS = 16
VMEM_LIMIT = 56 * 1024 * 1024
FF_ALIGN = 1024


def _tile(dim, pref, align):
    if dim <= pref:
        return dim
    t = (pref // align) * align
    while t >= align:
        if dim % t == 0:
            return t
        t -= align
    return dim


def _params(sem):
    return pltpu.CompilerParams(dimension_semantics=sem, vmem_limit_bytes=VMEM_LIMIT)


def _layer_vec(p, l):
    depth, n = p.shape
    return pl.BlockSpec((None, 1, n), lambda *_: (l, 0, 0)), p.reshape(depth, 1, n)


def _rms(x, g):
    return x * lax.rsqrt(jnp.mean(x * x, axis=-1, keepdims=True) + EPS) * g


def _rope(x, cos, sg):
    return x * cos + pltpu.roll(x, 2 * HALF_ROPE, 1) * sg


def _mm_kernel(x_ref, w_ref, o_ref, *, nk):
    part = jnp.dot(x_ref[...], w_ref[...], preferred_element_type=F32)
    if nk == 1:
        o_ref[...] = part.astype(o_ref.dtype)
        return
    k = pl.program_id(2)

    @pl.when(k == 0)
    def _():
        o_ref[...] = part

    @pl.when(k > 0)
    def _():
        o_ref[...] += part


def _matmul(x, w, out_dtype, *, bm, bn, bk, name, layer=None):
    m, kdim = x.shape
    n = w.shape[-1]
    bm, bn, bk = _tile(m, bm, 8), _tile(n, bn, LANES), _tile(kdim, bk, LANES)
    nk = kdim // bk
    assert nk == 1 or out_dtype == F32
    if layer is None:
        wspec = pl.BlockSpec((bk, bn), lambda i, j, k: (k, j))
    else:
        wspec = pl.BlockSpec((None, bk, bn), lambda i, j, k: (layer, k, j))
    return pl.pallas_call(
        functools.partial(_mm_kernel, nk=nk),
        out_shape=jax.ShapeDtypeStruct((m, n), out_dtype),
        grid=(m // bm, n // bn, nk),
        in_specs=[pl.BlockSpec((bm, bk), lambda i, j, k: (i, k)), wspec],
        out_specs=pl.BlockSpec((bm, bn), lambda i, j, k: (i, j)),
        compiler_params=_params(("parallel", "parallel", "arbitrary")),
        name=name,
    )(x, w)


def _rmsnorm_kernel(x_ref, g_ref, h_ref):
    h_ref[...] = _rms(x_ref[...], g_ref[...]).astype(h_ref.dtype)


def _rmsnorm(x, g, l, *, br=256):
    t, d = x.shape
    br = _tile(t, br, 8)
    gspec, g3 = _layer_vec(g, l)
    return pl.pallas_call(
        _rmsnorm_kernel,
        out_shape=jax.ShapeDtypeStruct((t, d), BF16),
        grid=(t // br,),
        in_specs=[pl.BlockSpec((br, d), lambda i: (i, 0)), gspec],
        out_specs=pl.BlockSpec((br, d), lambda i: (i, 0)),
        compiler_params=_params(("parallel",)),
        name="rmsnorm",
    )(x, g3)


def _resnorm_kernel(x_ref, d_ref, gpost_ref, gnext_ref, xo_ref, h_ref):
    xn = x_ref[...] + _rms(d_ref[...].astype(F32), gpost_ref[...])
    xo_ref[...] = xn
    h_ref[...] = _rms(xn, gnext_ref[...]).astype(h_ref.dtype)


def _resadd_kernel(x_ref, d_ref, gpost_ref, xo_ref):
    xo_ref[...] = x_ref[...] + _rms(d_ref[...].astype(F32), gpost_ref[...])


def _resnorm(x, delta, g_post, l, g_next=None, l_next=None, *, br=128):
    t, d = x.shape
    br = _tile(t, br, 8)
    row = pl.BlockSpec((br, d), lambda i: (i, 0))
    pspec, gp3 = _layer_vec(g_post, l)
    if g_next is None:
        return pl.pallas_call(
            _resadd_kernel,
            out_shape=jax.ShapeDtypeStruct((t, d), F32),
            grid=(t // br,),
            in_specs=[row, row, pspec],
            out_specs=row,
            compiler_params=_params(("parallel",)),
            name="resadd",
        )(x, delta, gp3), None
    nspec, gn3 = _layer_vec(g_next, l_next)
    return pl.pallas_call(
        _resnorm_kernel,
        out_shape=(jax.ShapeDtypeStruct((t, d), F32), jax.ShapeDtypeStruct((t, d), BF16)),
        grid=(t // br,),
        in_specs=[row, row, pspec, nspec],
        out_specs=(row, row),
        compiler_params=_params(("parallel",)),
        name="resnorm",
    )(x, delta, gp3, gn3)


def _qproj_kernel(c_ref, g_ref, w_ref, cos_ref, sg_ref, o_ref, xn_ref, *, heads):
    @pl.when(pl.program_id(1) == 0)
    def _():
        xn_ref[...] = _rms(c_ref[...].astype(F32), g_ref[...]).astype(BF16)

    acc = jnp.dot(xn_ref[...], w_ref[...], preferred_element_type=F32) * (ATTN_SCALE * LOG2E)
    cos, sg = cos_ref[...], sg_ref[...]
    for h in range(heads):
        lo = h * QK_PAD
        o_ref[:, lo:lo + QK_NOPE] = acc[:, lo:lo + QK_NOPE].astype(o_ref.dtype)
        o_ref[:, lo + QK_NOPE:lo + QK_PAD] = _rope(
            acc[:, lo + QK_NOPE:lo + QK_PAD], cos, sg).astype(o_ref.dtype)


def _qproj(u_lat, g, w, l, cos4, sg4, *, seq, q_lora, bm=1024, heads=4):
    t = u_lat.shape[0]
    n = w.shape[-1]
    bm = _tile(seq, bm, BF16_ROWS)
    heads = min(heads, n // QK_PAD)
    bn = heads * QK_PAD
    spb = seq // bm
    tab = pl.BlockSpec((bm, LANES), lambda i, j: (i % spb, 0))
    gspec, g3 = _layer_vec(g, l)
    return pl.pallas_call(
        functools.partial(_qproj_kernel, heads=heads),
        out_shape=jax.ShapeDtypeStruct((t, n), BF16),
        grid=(t // bm, n // bn),
        in_specs=[pl.BlockSpec((bm, q_lora), lambda i, j: (i, 0)),
                  gspec,
                  pl.BlockSpec((None, q_lora, bn), lambda i, j: (l, 0, j)),
                  tab, tab],
        out_specs=pl.BlockSpec((bm, bn), lambda i, j: (i, j)),
        scratch_shapes=[pltpu.VMEM((bm, q_lora), BF16)],
        compiler_params=_params(("parallel", "arbitrary")),
        name="qproj",
    )(u_lat, g3, w, cos4, sg4)


def _kvproj_kernel(c_ref, kr_ref, g_ref, w_ref, cos_ref, sg_ref, k_ref, v_ref, krr_ref,
                   xn_ref, *, heads):
    @pl.when(pl.program_id(1) == 0)
    def _():
        xn_ref[...] = _rms(c_ref[...].astype(F32), g_ref[...]).astype(BF16)
        krr_ref[...] = _rope(kr_ref[...].astype(F32), cos_ref[...], sg_ref[...]).astype(BF16)

    kv = jnp.dot(xn_ref[...], w_ref[...], preferred_element_type=F32)
    width = QK_NOPE + V_HEAD
    for h in range(heads):
        k_ref[:, h * QK_NOPE:(h + 1) * QK_NOPE] = kv[:, h * width:h * width + QK_NOPE].astype(
            k_ref.dtype)
        v_ref[:, h * V_HEAD:(h + 1) * V_HEAD] = kv[:, h * width + QK_NOPE:(h + 1) * width].astype(
            v_ref.dtype)


def _kvproj(u_lat, g, w, l, cos4, sg4, *, seq, q_lora, kv_lora, bm=1024, heads=4):
    t = u_lat.shape[0]
    width = QK_NOPE + V_HEAD
    n_heads = w.shape[-1] // width
    bm = _tile(seq, bm, BF16_ROWS)
    heads = min(heads, n_heads)
    spb = seq // bm
    assert q_lora % kv_lora == 0 and (q_lora + kv_lora) % LANES == 0
    ckv_blk = q_lora // kv_lora
    kr_blk = (q_lora + kv_lora) // LANES
    tab = pl.BlockSpec((bm, LANES), lambda i, j: (i % spb, 0))
    gspec, g3 = _layer_vec(g, l)
    return pl.pallas_call(
        functools.partial(_kvproj_kernel, heads=heads),
        out_shape=(jax.ShapeDtypeStruct((t, n_heads * QK_NOPE), BF16),
                   jax.ShapeDtypeStruct((t, n_heads * V_HEAD), BF16),
                   jax.ShapeDtypeStruct((t, LANES), BF16)),
        grid=(t // bm, n_heads // heads),
        in_specs=[pl.BlockSpec((bm, kv_lora), lambda i, j: (i, ckv_blk)),
                  pl.BlockSpec((bm, LANES), lambda i, j: (i, kr_blk)),
                  gspec,
                  pl.BlockSpec((None, kv_lora, heads * width), lambda i, j: (l, 0, j)),
                  tab, tab],
        out_specs=(pl.BlockSpec((bm, heads * QK_NOPE), lambda i, j: (i, j)),
                   pl.BlockSpec((bm, heads * V_HEAD), lambda i, j: (i, j)),
                   pl.BlockSpec((bm, LANES), lambda i, j: (i, 0))),
        scratch_shapes=[pltpu.VMEM((bm, kv_lora), BF16)],
        compiler_params=_params(("parallel", "arbitrary")),
        name="kvproj",
    )(u_lat, u_lat, g3, w, cos4, sg4)


def _attn_kernel(q_ref, kn_ref, kr_ref, v_ref, o_ref, k_scr, vt_scr, sa_scr, sb_scr, *, tk, nk,
                 group):
    @pl.when(pl.program_id(2) == 0)
    def _():
        def build(j, c):
            rows = pl.ds(pl.multiple_of(j * tk, tk), tk)
            k_scr[rows, :QK_NOPE] = kn_ref[rows, :]
            k_scr[rows, QK_NOPE:] = kr_ref[rows, :]
            vt_scr[:, rows] = v_ref[rows, :].T
            return c
        lax.fori_loop(0, nk, build, 0)

    qt = q_ref[...].T
    tq = qt.shape[1]

    def chunk(j):
        return pl.ds(j * tk if isinstance(j, int) else pl.multiple_of(j * tk, tk), tk)

    def scores(j, s_ref):
        s_ref[...] = jnp.dot(k_scr[chunk(j), :], qt, preferred_element_type=F32)

    def update(j, s_ref, carry):
        m, l, acc = carry
        m_new = jnp.maximum(m, jnp.max(s_ref[...], axis=0, keepdims=True))
        alpha = jnp.exp2(m - m_new)
        p = jnp.exp2(s_ref[...] - m_new)
        l = alpha * l + jnp.sum(p, axis=0, keepdims=True)
        acc = alpha * acc + jnp.dot(vt_scr[:, chunk(j)], p.astype(BF16),
                                    preferred_element_type=F32)
        return m_new, l, acc

    bufs = (sa_scr, sb_scr)

    def trip(i, carry):
        j0 = group * i
        for g in range(group):
            scores(j0 + g + 1, bufs[(g + 1) % 2])
            carry = update(j0 + g, bufs[g % 2], carry)
        return carry

    carry = (jnp.full((1, tq), -jnp.inf, F32), jnp.zeros((1, tq), F32),
             jnp.zeros((V_HEAD, tq), F32))
    scores(0, sa_scr)
    n_trips = (nk - 1) // group
    carry = lax.fori_loop(0, n_trips, trip, carry)
    for j in range(group * n_trips, nk):
        if j + 1 < nk:
            scores(j + 1, bufs[(j + 1) % 2])
        carry = update(j, bufs[j % 2], carry)
    _, l, acc = carry
    o_ref[...] = (acc / l).T.astype(o_ref.dtype)


def _attention(q, kn, krr, v, *, seq, tq=1024, tk=512, group=4):
    t = q.shape[0]
    n_heads = q.shape[1] // QK_PAD
    tq, tk = _tile(seq, tq, LANES), _tile(seq, tk, LANES)
    assert group % 2 == 0
    qpb = seq // tq
    return pl.pallas_call(
        functools.partial(_attn_kernel, tk=tk, nk=seq // tk, group=group),
        out_shape=jax.ShapeDtypeStruct((t, n_heads * V_HEAD), BF16),
        grid=(t // seq, n_heads, qpb),
        in_specs=[pl.BlockSpec((tq, QK_PAD), lambda b, h, i: (b * qpb + i, h)),
                  pl.BlockSpec((seq, QK_NOPE), lambda b, h, i: (b, h)),
                  pl.BlockSpec((seq, LANES), lambda b, h, i: (b, 0)),
                  pl.BlockSpec((seq, V_HEAD), lambda b, h, i: (b, h))],
        out_specs=pl.BlockSpec((tq, V_HEAD), lambda b, h, i: (b * qpb + i, h)),
        scratch_shapes=[pltpu.VMEM((seq, QK_PAD), BF16), pltpu.VMEM((V_HEAD, seq), BF16),
                        pltpu.VMEM((tk, tq), F32), pltpu.VMEM((tk, tq), F32)],
        compiler_params=_params(("parallel", "parallel", "arbitrary")),
        name="attention",
    )(q, kn, krr, v)


def _conv_kernel(a_ref, g_ref, ap_ref, gp_ref, an_ref, gn_ref, w_ref, b_ref, lg_ref, lb_ref,
                 o_ref, h_ref, c_ref, *, ts, tiles_per_seq, width, rows):
    i = pl.program_id(0)
    halo = BF16_ROWS

    def glu(a, g):
        return a[...].astype(F32) * jax.nn.sigmoid(g[...].astype(F32))

    h_ref[halo:halo + ts, :] = glu(a_ref, g_ref)
    first = (i % tiles_per_seq) == 0
    last = (i % tiles_per_seq) == tiles_per_seq - 1
    h_ref[0:halo, :] = jnp.where(first, 0.0, glu(ap_ref, gp_ref))
    h_ref[halo + ts:2 * halo + ts, :] = jnp.where(last, 0.0, glu(an_ref, gn_ref))

    pad = (width - 1) // 2
    base = halo - pad
    span = ((base + width - 1) // 8) * 8
    n_chunks = a_ref.shape[1] // LANES

    def lane_chunk(c, carry):
        lanes = pl.ds(pl.multiple_of(c * LANES, LANES), LANES)
        wts = [w_ref[pl.ds(k, 1), lanes] for k in range(width)]
        bias = b_ref[:, lanes]
        for r0 in range(0, ts, rows):
            acc = jnp.zeros((rows, LANES), F32)
            for r in range(8):
                taps = [k for k in range(width) if (base + k) % 8 == r]
                if not taps:
                    continue
                win = h_ref[pl.ds(r0 + r, rows + span), lanes]
                for k in taps:
                    off = (base + k) - r
                    acc = acc + wts[k] * win[off:off + rows, :]
            c_ref[pl.ds(r0, rows), lanes] = acc + bias
        return carry

    lax.fori_loop(0, n_chunks, lane_chunk, 0)

    y = c_ref[...]
    mu = jnp.mean(y, axis=-1, keepdims=True)
    yc = y - mu
    var = jnp.mean(yc * yc, axis=-1, keepdims=True)
    z = yc * lax.rsqrt(var + EPS) * lg_ref[...] + lb_ref[...]
    o_ref[...] = (z * jax.nn.sigmoid(z)).astype(o_ref.dtype)


def _conv_branch(u_main, conv_w, conv_b, ln_g, ln_b, l, *, seq, d_conv, ts=256, rows=64):
    t = u_main.shape[0]
    width = conv_w.shape[1]
    halo = BF16_ROWS
    assert (width - 1) // 2 <= halo
    ts = _tile(seq, ts, halo)
    rows = _tile(ts, rows, 8)
    hb = ts // halo
    last_blk = t // halo - 1
    cur = lambda c: pl.BlockSpec((ts, d_conv), lambda i: (i, c))
    prev = lambda c: pl.BlockSpec((halo, d_conv), lambda i: (jnp.maximum(i * hb - 1, 0), c))
    nxt = lambda c: pl.BlockSpec((halo, d_conv), lambda i: (jnp.minimum((i + 1) * hb, last_blk), c))
    bspec, b3 = _layer_vec(conv_b, l)
    gspec, g3 = _layer_vec(ln_g, l)
    lspec, l3 = _layer_vec(ln_b, l)
    return pl.pallas_call(
        functools.partial(_conv_kernel, ts=ts, tiles_per_seq=seq // ts, width=width, rows=rows),
        out_shape=jax.ShapeDtypeStruct((t, d_conv), BF16),
        grid=(t // ts,),
        in_specs=[cur(0), cur(1), prev(0), prev(1), nxt(0), nxt(1),
                  pl.BlockSpec((None, width, d_conv), lambda i: (l, 0, 0)), bspec, gspec, lspec],
        out_specs=pl.BlockSpec((ts, d_conv), lambda i: (i, 0)),
        scratch_shapes=[pltpu.VMEM((ts + 2 * halo, d_conv), F32),
                        pltpu.VMEM((ts, d_conv), F32)],
        compiler_params=_params(("parallel",)),
        name="conv_branch",
    )(u_main, u_main, u_main, u_main, u_main, u_main, conv_w, b3, g3, l3)


def _merge_kernel(hc_ref, o_ref, wc_ref, wo_ref, gc_ref, ga_ref, out_ref):
    yc = jnp.dot(hc_ref[...], wc_ref[...], preferred_element_type=F32)
    ya = jnp.dot(o_ref[...], wo_ref[...], preferred_element_type=F32)
    gc = jax.nn.sigmoid(gc_ref[...].astype(F32))
    ga = jax.nn.sigmoid(ga_ref[...].astype(F32))
    out_ref[...] = (gc * yc + ga * ya).astype(out_ref.dtype)


def _merge(hc, o, wc, wo, l, u_main, *, d_conv, bm=1024, bn=512):
    t = hc.shape[0]
    d = wc.shape[-1]
    bm, bn = _tile(t, bm, BF16_ROWS), _tile(d, bn, LANES)
    assert (2 * d_conv) % bn == 0
    g0 = 2 * d_conv // bn
    nb = d // bn
    return pl.pallas_call(
        _merge_kernel,
        out_shape=jax.ShapeDtypeStruct((t, d), BF16),
        grid=(t // bm, nb),
        in_specs=[pl.BlockSpec((bm, hc.shape[1]), lambda i, j: (i, 0)),
                  pl.BlockSpec((bm, o.shape[1]), lambda i, j: (i, 0)),
                  pl.BlockSpec((None, wc.shape[1], bn), lambda i, j: (l, 0, j)),
                  pl.BlockSpec((None, wo.shape[1], bn), lambda i, j: (l, 0, j)),
                  pl.BlockSpec((bm, bn), lambda i, j: (i, g0 + j)),
                  pl.BlockSpec((bm, bn), lambda i, j: (i, g0 + nb + j))],
        out_specs=pl.BlockSpec((bm, bn), lambda i, j: (i, j)),
        compiler_params=_params(("parallel", "parallel")),
        name="merge",
    )(hc, o, wc, wo, u_main, u_main)


def _gateup_kernel(h_ref, wg_ref, wu_ref, o_ref):
    h = h_ref[...]
    gate = jnp.dot(h, wg_ref[...], preferred_element_type=F32)
    up = jnp.dot(h, wu_ref[...], preferred_element_type=F32)
    o_ref[...] = (gate * jax.nn.sigmoid(gate) * up).astype(o_ref.dtype)


def _gateup(h, wg, wu, l, *, bm=1024, bn=512):
    t, d = h.shape
    f = wg.shape[-1]
    bm, bn = _tile(t, bm, BF16_ROWS), _tile(f, bn, LANES)
    wspec = pl.BlockSpec((None, d, bn), lambda i, j: (l, 0, j))
    return pl.pallas_call(
        _gateup_kernel,
        out_shape=jax.ShapeDtypeStruct((t, f), BF16),
        grid=(t // bm, f // bn),
        in_specs=[pl.BlockSpec((bm, d), lambda i, j: (i, 0)), wspec, wspec],
        out_specs=pl.BlockSpec((bm, bn), lambda i, j: (i, j)),
        compiler_params=_params(("parallel", "parallel")),
        name="gateup",
    )(h, wg, wu)


def _q_column_selector(n_heads):
    src = np.full((n_heads, QK_PAD), -1, np.int32)
    base = np.arange(n_heads)[:, None] * QK_HEAD
    src[:, :QK_NOPE + HALF_ROPE] = base + np.arange(QK_NOPE + HALF_ROPE)
    lo = QK_NOPE + 2 * HALF_ROPE
    src[:, lo:lo + HALF_ROPE] = base + QK_NOPE + HALF_ROPE + np.arange(HALF_ROPE)
    rows = lax.broadcasted_iota(jnp.int32, (n_heads * QK_HEAD, n_heads * QK_PAD), 0)
    return (rows == jnp.asarray(src.reshape(1, -1))).astype(BF16)


def _prep_weights(w_in, w_conv_out, w_uq, w_ukv, w_o_attn, w_out, w_gate, w_up, w_down,
                  *, d_conv, q_lora, kv_lora):
    depth, d, _ = w_in.shape
    off_cq = 2 * d_conv
    off_ckv = off_cq + q_lora
    off_kr = off_ckv + kv_lora
    off_gate = off_kr + QK_ROPE
    zr = jnp.zeros((depth, d, HALF_ROPE), w_in.dtype)
    w_main = jnp.concatenate([w_in[..., :off_cq], w_in[..., off_gate:]], axis=-1).astype(BF16)
    w_lat = jnp.concatenate([w_in[..., off_cq:off_kr],
                             w_in[..., off_kr:off_kr + HALF_ROPE], zr,
                             w_in[..., off_kr + HALF_ROPE:off_gate], zr], axis=-1).astype(BF16)

    n_heads = w_uq.shape[-1] // QK_HEAD
    wq = _matmul(w_uq.astype(BF16).reshape(depth * q_lora, n_heads * QK_HEAD),
                 _q_column_selector(n_heads), BF16, bm=512, bn=512, bk=n_heads * QK_HEAD,
                 name="wq_layout").reshape(depth, q_lora, n_heads * QK_PAD)

    d_ff = w_gate.shape[-1]
    fp = -(-d_ff // FF_ALIGN) * FF_ALIGN if d_ff > FF_ALIGN else d_ff
    padc = lambda w: jnp.pad(w.astype(BF16), ((0, 0), (0, 0), (0, fp - d_ff)))
    w_dn = jnp.pad(w_down.astype(BF16), ((0, 0), (0, fp - d_ff), (0, 0)))

    return dict(w_main=w_main, w_lat=w_lat, wq=wq, wkv=w_ukv.astype(BF16),
                wc=w_conv_out.astype(BF16), wo=w_o_attn.astype(BF16), w_out=w_out.astype(BF16),
                wg=padc(w_gate), wu=padc(w_up), w_dn=w_dn)


def _rope_tables(seq):
    inv_freq = ROPE_THETA ** (-jnp.arange(0, QK_ROPE, 2, dtype=F32) / QK_ROPE)
    ang = jnp.arange(seq, dtype=F32)[:, None] * inv_freq[None, :]
    cos, sin = jnp.cos(ang), jnp.sin(ang)
    return (jnp.concatenate([cos, cos, cos, cos], axis=-1),
            jnp.concatenate([-sin, -sin, sin, sin], axis=-1))


def _trunk(x3, pw, small, dims):
    b, seq, d = x3.shape
    x = x3.reshape(b * seq, d)
    depth = pw["w_main"].shape[0]
    d_conv, q_lora, kv_lora = dims
    cos4, sg4 = _rope_tables(seq)
    h = _rmsnorm(x, small["pre_norm_mix"], 0)
    for l in range(depth):
        u_main = _matmul(h, pw["w_main"], BF16, bm=1024, bn=1024, bk=d, name="w_in_main", layer=l)
        u_lat = _matmul(h, pw["w_lat"], BF16, bm=512, bn=pw["w_lat"].shape[-1], bk=d,
                        name="w_in_latent", layer=l)
        hc = _conv_branch(u_main, small["conv_w"], small["conv_b"], small["conv_ln_g"],
                          small["conv_ln_b"], l, seq=seq, d_conv=d_conv)
        q = _qproj(u_lat, small["q_norm"], pw["wq"], l, cos4, sg4, seq=seq, q_lora=q_lora)
        kn, v, krr = _kvproj(u_lat, small["kv_norm"], pw["wkv"], l, cos4, sg4,
                             seq=seq, q_lora=q_lora, kv_lora=kv_lora)
        o = _attention(q, kn, krr, v, seq=seq)
        mix = _merge(hc, o, pw["wc"], pw["wo"], l, u_main, d_conv=d_conv)
        mixed = _matmul(mix, pw["w_out"], F32, bm=1024, bn=1024, bk=d, name="w_out", layer=l)
        x, h = _resnorm(x, mixed, small["post_norm_mix"], l, small["pre_norm_ffn"], l)
        act = _gateup(h, pw["wg"], pw["wu"], l)
        f = _matmul(act, pw["w_dn"], F32, bm=1024, bn=1024, bk=3072, name="w_down", layer=l)
        if l + 1 < depth:
            x, h = _resnorm(x, f, small["post_norm_ffn"], l, small["pre_norm_mix"], l + 1)
        else:
            x, _ = _resnorm(x, f, small["post_norm_ffn"], l)
    return x.reshape(b, seq, d)


def kernel(x_prompt, x_sample, pre_norm_mix, w_in, conv_w, conv_b, conv_ln_g, conv_ln_b, w_conv_out, q_norm, w_uq, kv_norm, w_ukv, w_o_attn, w_out, post_norm_mix, pre_norm_ffn, w_gate, w_up, w_down, post_norm_ffn):
    d_conv = conv_w.shape[-1]
    q_lora = q_norm.shape[-1]
    kv_lora = kv_norm.shape[-1]
    pw = _prep_weights(w_in, w_conv_out, w_uq, w_ukv, w_o_attn, w_out, w_gate, w_up, w_down,
                       d_conv=d_conv, q_lora=q_lora, kv_lora=kv_lora)
    small = dict(pre_norm_mix=pre_norm_mix, conv_w=conv_w, conv_b=conv_b, conv_ln_g=conv_ln_g,
                 conv_ln_b=conv_ln_b, q_norm=q_norm, kv_norm=kv_norm, post_norm_mix=post_norm_mix,
                 pre_norm_ffn=pre_norm_ffn, post_norm_ffn=post_norm_ffn)
    dims = (d_conv, q_lora, kv_lora)
    return (_trunk(x_prompt, pw, small, dims), _trunk(x_sample, pw, small, dims))
```

```python
import functools

import numpy as np
import jax
import jax.numpy as jnp
from jax import lax
from jax.experimental import pallas as pl
from jax.experimental.pallas import tpu as pltpu

F32 = jnp.float32
BF16 = jnp.bfloat16

QK_NOPE = 128
QK_ROPE = 64
QK_HEAD = QK_NOPE + QK_ROPE
V_HEAD = 128
ROPE_THETA = 10000.0
ATTN_SCALE = QK_HEAD ** -0.5
LOG2E = 1.4426950408889634
EPS = 1e-6

LANES = 128
HALF_ROPE = QK_ROPE // 2
QK_PAD = QK_NOPE + LANES
BF16_ROWS = 16
VMEM_LIMIT = 56 * 1024 * 1024
FF_ALIGN = 1024


def _tile(dim, pref, align):
    if dim <= pref:
        return dim
    t = (pref // align) * align
    while t >= align:
        if dim % t == 0:
            return t
        t -= align
    return dim


def _params(sem):
    return pltpu.CompilerParams(dimension_semantics=sem, vmem_limit_bytes=VMEM_LIMIT)


def _layer_vec(p, l):
    depth, n = p.shape
    return pl.BlockSpec((None, 1, n), lambda *_: (l, 0, 0)), p.reshape(depth, 1, n)


def _rms(x, g):
    return x * lax.rsqrt(jnp.mean(x * x, axis=-1, keepdims=True) + EPS) * g


def _rope(x, cos, sg):
    return x * cos + pltpu.roll(x, 2 * HALF_ROPE, 1) * sg


def _mm_kernel(x_ref, w_ref, o_ref, *scratch, nk):
    part = jnp.dot(x_ref[...], w_ref[...], preferred_element_type=F32)
    if nk == 1:
        o_ref[...] = part.astype(o_ref.dtype)
        return
    acc_ref = scratch[0] if scratch else o_ref
    k = pl.program_id(2)

    @pl.when(k == 0)
    def _():
        acc_ref[...] = part

    @pl.when(k > 0)
    def _():
        acc_ref[...] += part

    if scratch:
        @pl.when(k == nk - 1)
        def _():
            o_ref[...] = acc_ref[...].astype(o_ref.dtype)


def _matmul(x, w, out_dtype, *, bm, bn, bk, name, layer=None):
    m, kdim = x.shape
    n = w.shape[-1]
    bm, bn, bk = _tile(m, bm, 8), _tile(n, bn, LANES), _tile(kdim, bk, LANES)
    nk = kdim // bk
    scratch = [pltpu.VMEM((bm, bn), F32)] if nk > 1 and out_dtype != F32 else []
    if layer is None:
        wspec = pl.BlockSpec((bk, bn), lambda i, j, k: (k, j))
    else:
        wspec = pl.BlockSpec((None, bk, bn), lambda i, j, k: (layer, k, j))
    return pl.pallas_call(
        functools.partial(_mm_kernel, nk=nk),
        out_shape=jax.ShapeDtypeStruct((m, n), out_dtype),
        grid=(m // bm, n // bn, nk),
        in_specs=[pl.BlockSpec((bm, bk), lambda i, j, k: (i, k)), wspec],
        out_specs=pl.BlockSpec((bm, bn), lambda i, j, k: (i, j)),
        scratch_shapes=scratch,
        compiler_params=_params(("parallel", "parallel", "arbitrary")),
        name=name,
    )(x, w)


def _rmsnorm_kernel(x_ref, g_ref, h_ref):
    h_ref[...] = _rms(x_ref[...], g_ref[...]).astype(h_ref.dtype)


def _rmsnorm(x, g, l, *, br=256):
    t, d = x.shape
    br = _tile(t, br, 8)
    gspec, g3 = _layer_vec(g, l)
    return pl.pallas_call(
        _rmsnorm_kernel,
        out_shape=jax.ShapeDtypeStruct((t, d), BF16),
        grid=(t // br,),
        in_specs=[pl.BlockSpec((br, d), lambda i: (i, 0)), gspec],
        out_specs=pl.BlockSpec((br, d), lambda i: (i, 0)),
        compiler_params=_params(("parallel",)),
        name="rmsnorm",
    )(x, g3)


def _resnorm_kernel(x_ref, d_ref, gpost_ref, gnext_ref, xo_ref, h_ref):
    xn = x_ref[...] + _rms(d_ref[...].astype(F32), gpost_ref[...])
    xo_ref[...] = xn
    h_ref[...] = _rms(xn, gnext_ref[...]).astype(h_ref.dtype)


def _resadd_kernel(x_ref, d_ref, gpost_ref, xo_ref):
    xo_ref[...] = x_ref[...] + _rms(d_ref[...].astype(F32), gpost_ref[...])


def _resnorm(x, delta, g_post, l, g_next=None, l_next=None, *, br=128):
    t, d = x.shape
    br = _tile(t, br, 8)
    row = pl.BlockSpec((br, d), lambda i: (i, 0))
    pspec, gp3 = _layer_vec(g_post, l)
    if g_next is None:
        return pl.pallas_call(
            _resadd_kernel,
            out_shape=jax.ShapeDtypeStruct((t, d), F32),
            grid=(t // br,),
            in_specs=[row, row, pspec],
            out_specs=row,
            compiler_params=_params(("parallel",)),
            name="resadd",
        )(x, delta, gp3), None
    nspec, gn3 = _layer_vec(g_next, l_next)
    return pl.pallas_call(
        _resnorm_kernel,
        out_shape=(jax.ShapeDtypeStruct((t, d), F32), jax.ShapeDtypeStruct((t, d), BF16)),
        grid=(t // br,),
        in_specs=[row, row, pspec, nspec],
        out_specs=(row, row),
        compiler_params=_params(("parallel",)),
        name="resnorm",
    )(x, delta, gp3, gn3)


def _qproj_kernel(c_ref, g_ref, w_ref, cos_ref, sg_ref, o_ref, xn_ref, *, heads):
    @pl.when(pl.program_id(1) == 0)
    def _():
        xn_ref[...] = _rms(c_ref[...].astype(F32), g_ref[...]).astype(BF16)

    acc = jnp.dot(xn_ref[...], w_ref[...], preferred_element_type=F32) * (ATTN_SCALE * LOG2E)
    cos, sg = cos_ref[...], sg_ref[...]
    for h in range(heads):
        lo = h * QK_PAD
        o_ref[:, lo:lo + QK_NOPE] = acc[:, lo:lo + QK_NOPE].astype(o_ref.dtype)
        o_ref[:, lo + QK_NOPE:lo + QK_PAD] = _rope(
            acc[:, lo + QK_NOPE:lo + QK_PAD], cos, sg).astype(o_ref.dtype)


def _qproj(u_lat, g, w, l, cos4, sg4, *, seq, q_lora, bm=1024, heads=4):
    t = u_lat.shape[0]
    n = w.shape[-1]
    bm = _tile(seq, bm, BF16_ROWS)
    heads = min(heads, n // QK_PAD)
    bn = heads * QK_PAD
    spb = seq // bm
    tab = pl.BlockSpec((bm, LANES), lambda i, j: (i % spb, 0))
    gspec, g3 = _layer_vec(g, l)
    return pl.pallas_call(
        functools.partial(_qproj_kernel, heads=heads),
        out_shape=jax.ShapeDtypeStruct((t, n), BF16),
        grid=(t // bm, n // bn),
        in_specs=[pl.BlockSpec((bm, q_lora), lambda i, j: (i, 0)),
                  gspec,
                  pl.BlockSpec((None, q_lora, bn), lambda i, j: (l, 0, j)),
                  tab, tab],
        out_specs=pl.BlockSpec((bm, bn), lambda i, j: (i, j)),
        scratch_shapes=[pltpu.VMEM((bm, q_lora), BF16)],
        compiler_params=_params(("parallel", "arbitrary")),
        name="qproj",
    )(u_lat, g3, w, cos4, sg4)


def _kvproj_kernel(c_ref, kr_ref, g_ref, w_ref, cos_ref, sg_ref, k_ref, v_ref, krr_ref,
                   xn_ref, *, heads):
    @pl.when(pl.program_id(1) == 0)
    def _():
        xn_ref[...] = _rms(c_ref[...].astype(F32), g_ref[...]).astype(BF16)
        krr_ref[...] = _rope(kr_ref[...].astype(F32), cos_ref[...], sg_ref[...]).astype(BF16)

    kv = jnp.dot(xn_ref[...], w_ref[...], preferred_element_type=F32)
    width = QK_NOPE + V_HEAD
    for h in range(heads):
        k_ref[:, h * QK_NOPE:(h + 1) * QK_NOPE] = kv[:, h * width:h * width + QK_NOPE].astype(
            k_ref.dtype)
        v_ref[:, h * V_HEAD:(h + 1) * V_HEAD] = kv[:, h * width + QK_NOPE:(h + 1) * width].astype(
            v_ref.dtype)


def _kvproj(u_lat, g, w, l, cos4, sg4, *, seq, q_lora, kv_lora, bm=1024, heads=4):
    t = u_lat.shape[0]
    width = QK_NOPE + V_HEAD
    n_heads = w.shape[-1] // width
    bm = _tile(seq, bm, BF16_ROWS)
    heads = min(heads, n_heads)
    spb = seq // bm
    assert q_lora % kv_lora == 0 and (q_lora + kv_lora) % LANES == 0
    ckv_blk = q_lora // kv_lora
    kr_blk = (q_lora + kv_lora) // LANES
    tab = pl.BlockSpec((bm, LANES), lambda i, j: (i % spb, 0))
    gspec, g3 = _layer_vec(g, l)
    return pl.pallas_call(
        functools.partial(_kvproj_kernel, heads=heads),
        out_shape=(jax.ShapeDtypeStruct((t, n_heads * QK_NOPE), BF16),
                   jax.ShapeDtypeStruct((t, n_heads * V_HEAD), BF16),
                   jax.ShapeDtypeStruct((t, LANES), BF16)),
        grid=(t // bm, n_heads // heads),
        in_specs=[pl.BlockSpec((bm, kv_lora), lambda i, j: (i, ckv_blk)),
                  pl.BlockSpec((bm, LANES), lambda i, j: (i, kr_blk)),
                  gspec,
                  pl.BlockSpec((None, kv_lora, heads * width), lambda i, j: (l, 0, j)),
                  tab, tab],
        out_specs=(pl.BlockSpec((bm, heads * QK_NOPE), lambda i, j: (i, j)),
                   pl.BlockSpec((bm, heads * V_HEAD), lambda i, j: (i, j)),
                   pl.BlockSpec((bm, LANES), lambda i, j: (i, 0))),
        scratch_shapes=[pltpu.VMEM((bm, kv_lora), BF16)],
        compiler_params=_params(("parallel", "arbitrary")),
        name="kvproj",
    )(u_lat, u_lat, g3, w, cos4, sg4)


def _attn_kernel(q_ref, kn_ref, kr_ref, v_ref, o_ref, k_scr, vt_scr, sa_scr, sb_scr, *, tk, nk,
                 group):
    @pl.when(pl.program_id(2) == 0)
    def _():
        def build(j, c):
            rows = pl.ds(pl.multiple_of(j * tk, tk), tk)
            k_scr[rows, :QK_NOPE] = kn_ref[rows, :]
            k_scr[rows, QK_NOPE:] = kr_ref[rows, :]
            vt_scr[:, rows] = v_ref[rows, :].T
            return c
        lax.fori_loop(0, nk, build, 0)

    qt = q_ref[...].T
    tq = qt.shape[1]

    def chunk(j):
        return pl.ds(j * tk if isinstance(j, int) else pl.multiple_of(j * tk, tk), tk)

    def scores(j, s_ref):
        s_ref[...] = jnp.dot(k_scr[chunk(j), :], qt, preferred_element_type=F32)

    def update(j, s_ref, carry):
        m, l, acc = carry
        m_new = jnp.maximum(m, jnp.max(s_ref[...], axis=0, keepdims=True))
        alpha = jnp.exp2(m - m_new)
        p = jnp.exp2(s_ref[...] - m_new)
        l = alpha * l + jnp.sum(p, axis=0, keepdims=True)
        acc = alpha * acc + jnp.dot(vt_scr[:, chunk(j)], p.astype(BF16),
                                    preferred_element_type=F32)
        return m_new, l, acc

    bufs = (sa_scr, sb_scr)

    def trip(i, carry):
        j0 = group * i
        for g in range(group):
            scores(j0 + g + 1, bufs[(g + 1) % 2])
            carry = update(j0 + g, bufs[g % 2], carry)
        return carry

    carry = (jnp.full((1, tq), -jnp.inf, F32), jnp.zeros((1, tq), F32),
             jnp.zeros((V_HEAD, tq), F32))
    scores(0, sa_scr)
    n_trips = (nk - 1) // group
    carry = lax.fori_loop(0, n_trips, trip, carry)
    for j in range(group * n_trips, nk):
        if j + 1 < nk:
            scores(j + 1, bufs[(j + 1) % 2])
        carry = update(j, bufs[j % 2], carry)
    _, l, acc = carry
    o_ref[...] = (acc / l).T.astype(o_ref.dtype)


def _attention(q, kn, krr, v, *, seq, tq=1024, tk=512, group=4):
    t = q.shape[0]
    n_heads = q.shape[1] // QK_PAD
    tq, tk = _tile(seq, tq, LANES), _tile(seq, tk, LANES)
    assert group % 2 == 0
    qpb = seq // tq
    return pl.pallas_call(
        functools.partial(_attn_kernel, tk=tk, nk=seq // tk, group=group),
        out_shape=jax.ShapeDtypeStruct((t, n_heads * V_HEAD), BF16),
        grid=(t // seq, n_heads, qpb),
        in_specs=[pl.BlockSpec((tq, QK_PAD), lambda b, h, i: (b * qpb + i, h)),
                  pl.BlockSpec((seq, QK_NOPE), lambda b, h, i: (b, h)),
                  pl.BlockSpec((seq, LANES), lambda b, h, i: (b, 0)),
                  pl.BlockSpec((seq, V_HEAD), lambda b, h, i: (b, h))],
        out_specs=pl.BlockSpec((tq, V_HEAD), lambda b, h, i: (b * qpb + i, h)),
        scratch_shapes=[pltpu.VMEM((seq, QK_PAD), BF16), pltpu.VMEM((V_HEAD, seq), BF16),
                        pltpu.VMEM((tk, tq), F32), pltpu.VMEM((tk, tq), F32)],
        compiler_params=_params(("parallel", "parallel", "arbitrary")),
        name="attention",
    )(q, kn, krr, v)


def _conv_kernel(a_ref, g_ref, ap_ref, gp_ref, an_ref, gn_ref, w_ref, b_ref, lg_ref, lb_ref,
                 o_ref, h_ref, c_ref, *, ts, tiles_per_seq, width, rows):
    i = pl.program_id(0)
    halo = BF16_ROWS

    def glu(a, g):
        return a[...].astype(F32) * jax.nn.sigmoid(g[...].astype(F32))

    h_ref[halo:halo + ts, :] = glu(a_ref, g_ref)
    first = (i % tiles_per_seq) == 0
    last = (i % tiles_per_seq) == tiles_per_seq - 1
    h_ref[0:halo, :] = jnp.where(first, 0.0, glu(ap_ref, gp_ref))
    h_ref[halo + ts:2 * halo + ts, :] = jnp.where(last, 0.0, glu(an_ref, gn_ref))

    pad = (width - 1) // 2
    base = halo - pad
    span = ((base + width - 1) // 8) * 8
    wrows = rows + span + 8
    assert wrows <= rows + 2 * halo
    n_chunks = a_ref.shape[1] // LANES

    def lane_chunk(c, carry):
        lanes = pl.ds(pl.multiple_of(c * LANES, LANES), LANES)
        wts = [w_ref[pl.ds(k, 1), lanes] for k in range(width)]
        bias = b_ref[:, lanes]
        for r0 in range(0, ts, rows):
            aligned = h_ref[pl.ds(r0, wrows), lanes]
            acc = jnp.zeros((rows, LANES), F32)
            for r in range(8):
                taps = [k for k in range(width) if (base + k) % 8 == r]
                if not taps:
                    continue
                win = aligned if r == 0 else pltpu.roll(aligned, wrows - r, 0)
                for k in taps:
                    off = (base + k) - r
                    acc = acc + wts[k] * win[off:off + rows, :]
            c_ref[pl.ds(r0, rows), lanes] = acc + bias
        return carry

    lax.fori_loop(0, n_chunks, lane_chunk, 0)

    y = c_ref[...]
    mu = jnp.mean(y, axis=-1, keepdims=True)
    yc = y - mu
    var = jnp.mean(yc * yc, axis=-1, keepdims=True)
    z = yc * lax.rsqrt(var + EPS) * lg_ref[...] + lb_ref[...]
    o_ref[...] = (z * jax.nn.sigmoid(z)).astype(o_ref.dtype)


def _conv_branch(u_main, conv_w, conv_b, ln_g, ln_b, l, *, seq, d_conv, ts=256, rows=64):
    t = u_main.shape[0]
    width = conv_w.shape[1]
    halo = BF16_ROWS
    assert (width - 1) // 2 <= halo
    ts = _tile(seq, ts, halo)
    rows = _tile(ts, rows, 8)
    hb = ts // halo
    last_blk = t // halo - 1
    cur = lambda c: pl.BlockSpec((ts, d_conv), lambda i: (i, c))
    prev = lambda c: pl.BlockSpec((halo, d_conv), lambda i: (jnp.maximum(i * hb - 1, 0), c))
    nxt = lambda c: pl.BlockSpec((halo, d_conv), lambda i: (jnp.minimum((i + 1) * hb, last_blk), c))
    bspec, b3 = _layer_vec(conv_b, l)
    gspec, g3 = _layer_vec(ln_g, l)
    lspec, l3 = _layer_vec(ln_b, l)
    return pl.pallas_call(
        functools.partial(_conv_kernel, ts=ts, tiles_per_seq=seq // ts, width=width, rows=rows),
        out_shape=jax.ShapeDtypeStruct((t, d_conv), BF16),
        grid=(t // ts,),
        in_specs=[cur(0), cur(1), prev(0), prev(1), nxt(0), nxt(1),
                  pl.BlockSpec((None, width, d_conv), lambda i: (l, 0, 0)), bspec, gspec, lspec],
        out_specs=pl.BlockSpec((ts, d_conv), lambda i: (i, 0)),
        scratch_shapes=[pltpu.VMEM((ts + 2 * halo, d_conv), F32),
                        pltpu.VMEM((ts, d_conv), F32)],
        compiler_params=_params(("parallel",)),
        name="conv_branch",
    )(u_main, u_main, u_main, u_main, u_main, u_main, conv_w, b3, g3, l3)


def _merge_kernel(hc_ref, o_ref, wc_ref, wo_ref, gc_ref, ga_ref, out_ref):
    yc = jnp.dot(hc_ref[...], wc_ref[...], preferred_element_type=F32)
    ya = jnp.dot(o_ref[...], wo_ref[...], preferred_element_type=F32)
    gc = jax.nn.sigmoid(gc_ref[...].astype(F32))
    ga = jax.nn.sigmoid(ga_ref[...].astype(F32))
    out_ref[...] = (gc * yc + ga * ya).astype(out_ref.dtype)


def _merge(hc, o, wc, wo, l, u_main, *, d_conv, bm=1024, bn=512):
    t = hc.shape[0]
    d = wc.shape[-1]
    bm, bn = _tile(t, bm, BF16_ROWS), _tile(d, bn, LANES)
    assert (2 * d_conv) % bn == 0
    g0 = 2 * d_conv // bn
    nb = d // bn
    return pl.pallas_call(
        _merge_kernel,
        out_shape=jax.ShapeDtypeStruct((t, d), BF16),
        grid=(t // bm, nb),
        in_specs=[pl.BlockSpec((bm, hc.shape[1]), lambda i, j: (i, 0)),
                  pl.BlockSpec((bm, o.shape[1]), lambda i, j: (i, 0)),
                  pl.BlockSpec((None, wc.shape[1], bn), lambda i, j: (l, 0, j)),
                  pl.BlockSpec((None, wo.shape[1], bn), lambda i, j: (l, 0, j)),
                  pl.BlockSpec((bm, bn), lambda i, j: (i, g0 + j)),
                  pl.BlockSpec((bm, bn), lambda i, j: (i, g0 + nb + j))],
        out_specs=pl.BlockSpec((bm, bn), lambda i, j: (i, j)),
        compiler_params=_params(("parallel", "parallel")),
        name="merge",
    )(hc, o, wc, wo, u_main, u_main)


def _gateup_kernel(h_ref, wg_ref, wu_ref, o_ref):
    h = h_ref[...]
    gate = jnp.dot(h, wg_ref[...], preferred_element_type=F32)
    up = jnp.dot(h, wu_ref[...], preferred_element_type=F32)
    o_ref[...] = (gate * jax.nn.sigmoid(gate) * up).astype(o_ref.dtype)


def _gateup(h, wg, wu, l, *, bm=1024, bn=512):
    t, d = h.shape
    f = wg.shape[-1]
    bm, bn = _tile(t, bm, BF16_ROWS), _tile(f, bn, LANES)
    wspec = pl.BlockSpec((None, d, bn), lambda i, j: (l, 0, j))
    return pl.pallas_call(
        _gateup_kernel,
        out_shape=jax.ShapeDtypeStruct((t, f), BF16),
        grid=(t // bm, f // bn),
        in_specs=[pl.BlockSpec((bm, d), lambda i, j: (i, 0)), wspec, wspec],
        out_specs=pl.BlockSpec((bm, bn), lambda i, j: (i, j)),
        compiler_params=_params(("parallel", "parallel")),
        name="gateup",
    )(h, wg, wu)


def _q_column_selector(n_heads):
    src = np.full((n_heads, QK_PAD), -1, np.int32)
    base = np.arange(n_heads)[:, None] * QK_HEAD
    src[:, :QK_NOPE + HALF_ROPE] = base + np.arange(QK_NOPE + HALF_ROPE)
    lo = QK_NOPE + 2 * HALF_ROPE
    src[:, lo:lo + HALF_ROPE] = base + QK_NOPE + HALF_ROPE + np.arange(HALF_ROPE)
    rows = lax.broadcasted_iota(jnp.int32, (n_heads * QK_HEAD, n_heads * QK_PAD), 0)
    return (rows == jnp.asarray(src.reshape(1, -1))).astype(BF16)


def _prep_weights(w_in, w_conv_out, w_uq, w_ukv, w_o_attn, w_out, w_gate, w_up, w_down,
                  *, d_conv, q_lora, kv_lora):
    depth, d, _ = w_in.shape
    off_cq = 2 * d_conv
    off_ckv = off_cq + q_lora
    off_kr = off_ckv + kv_lora
    off_gate = off_kr + QK_ROPE
    zr = jnp.zeros((depth, d, HALF_ROPE), w_in.dtype)
    w_main = jnp.concatenate([w_in[..., :off_cq], w_in[..., off_gate:]], axis=-1).astype(BF16)
    w_lat = jnp.concatenate([w_in[..., off_cq:off_kr],
                             w_in[..., off_kr:off_kr + HALF_ROPE], zr,
                             w_in[..., off_kr + HALF_ROPE:off_gate], zr], axis=-1).astype(BF16)

    n_heads = w_uq.shape[-1] // QK_HEAD
    wq = _matmul(w_uq.astype(BF16).reshape(depth * q_lora, n_heads * QK_HEAD),
                 _q_column_selector(n_heads), BF16, bm=512, bn=512, bk=n_heads * QK_HEAD,
                 name="wq_layout").reshape(depth, q_lora, n_heads * QK_PAD)

    d_ff = w_gate.shape[-1]
    fp = -(-d_ff // FF_ALIGN) * FF_ALIGN if d_ff > FF_ALIGN else d_ff
    padc = lambda w: jnp.pad(w.astype(BF16), ((0, 0), (0, 0), (0, fp - d_ff)))
    w_dn = jnp.pad(w_down.astype(BF16), ((0, 0), (0, fp - d_ff), (0, 0)))

    return dict(w_main=w_main, w_lat=w_lat, wq=wq, wkv=w_ukv.astype(BF16),
                wc=w_conv_out.astype(BF16), wo=w_o_attn.astype(BF16), w_out=w_out.astype(BF16),
                wg=padc(w_gate), wu=padc(w_up), w_dn=w_dn)


def _rope_tables(seq):
    inv_freq = ROPE_THETA ** (-jnp.arange(0, QK_ROPE, 2, dtype=F32) / QK_ROPE)
    ang = jnp.arange(seq, dtype=F32)[:, None] * inv_freq[None, :]
    cos, sin = jnp.cos(ang), jnp.sin(ang)
    return (jnp.concatenate([cos, cos, cos, cos], axis=-1),
            jnp.concatenate([-sin, -sin, sin, sin], axis=-1))


def _trunk(x3, pw, small, dims):
    b, seq, d = x3.shape
    x = x3.reshape(b * seq, d)
    depth = pw["w_main"].shape[0]
    d_conv, q_lora, kv_lora = dims
    cos4, sg4 = _rope_tables(seq)
    h = _rmsnorm(x, small["pre_norm_mix"], 0)
    for l in range(depth):
        u_main = _matmul(h, pw["w_main"], BF16, bm=1024, bn=1024, bk=d, name="w_in_main", layer=l)
        u_lat = _matmul(h, pw["w_lat"], BF16, bm=512, bn=pw["w_lat"].shape[-1], bk=d,
                        name="w_in_latent", layer=l)
        hc = _conv_branch(u_main, small["conv_w"], small["conv_b"], small["conv_ln_g"],
                          small["conv_ln_b"], l, seq=seq, d_conv=d_conv)
        q = _qproj(u_lat, small["q_norm"], pw["wq"], l, cos4, sg4, seq=seq, q_lora=q_lora)
        kn, v, krr = _kvproj(u_lat, small["kv_norm"], pw["wkv"], l, cos4, sg4,
                             seq=seq, q_lora=q_lora, kv_lora=kv_lora)
        o = _attention(q, kn, krr, v, seq=seq)
        mix = _merge(hc, o, pw["wc"], pw["wo"], l, u_main, d_conv=d_conv)
        mixed = _matmul(mix, pw["w_out"], BF16, bm=1024, bn=1024, bk=d, name="w_out", layer=l)
        x, h = _resnorm(x, mixed, small["post_norm_mix"], l, small["pre_norm_ffn"], l)
        act = _gateup(h, pw["wg"], pw["wu"], l)
        f = _matmul(act, pw["w_dn"], BF16, bm=1024, bn=1024, bk=3072, name="w_down", layer=l)
        if l + 1 < depth:
            x, h = _resnorm(x, f, small["post_norm_ffn"], l, small["pre_norm_mix"], l + 1)
        else:
            x, _ = _resnorm(x, f, small["post_norm_ffn"], l)
    return x.reshape(b, seq, d)


def kernel(x_prompt, x_sample, pre_norm_mix, w_in, conv_w, conv_b, conv_ln_g, conv_ln_b, w_conv_out, q_norm, w_uq, kv_norm, w_ukv, w_o_attn, w_out, post_norm_mix, pre_norm_ffn, w_gate, w_up, w_down, post_norm_ffn):
    d_conv = conv_w.shape[-1]
    q_lora = q_norm.shape[-1]
    kv_lora = kv_norm.shape[-1]
    pw = _prep_weights(w_in, w_conv_out, w_uq, w_ukv, w_o_attn, w_out, w_gate, w_up, w_down,
                       d_conv=d_conv, q_lora=q_lora, kv_lora=kv_lora)
    small = dict(pre_norm_mix=pre_norm_mix, conv_w=conv_w, conv_b=conv_b, conv_ln_g=conv_ln_g,
                 conv_ln_b=conv_ln_b, q_norm=q_norm, kv_norm=kv_norm, post_norm_mix=post_norm_mix,
                 pre_norm_ffn=pre_norm_ffn, post_norm_ffn=post_norm_ffn)
    dims = (d_conv, q_lora, kv_lora)
    return (_trunk(x_prompt, pw, small, dims), _trunk(x_sample, pw, small, dims))
```

```python
import functools

import numpy as np
import jax
import jax.numpy as jnp
from jax import lax
from jax.experimental import pallas as pl
from jax.experimental.pallas import tpu as pltpu

F32 = jnp.float32
BF16 = jnp.bfloat16

QK_NOPE = 128
QK_ROPE = 64
QK_HEAD = QK_NOPE + QK_ROPE
V_HEAD = 128
ROPE_THETA = 10000.0
ATTN_SCALE = QK_HEAD ** -0.5
LOG2E = 1.4426950408889634
EPS = 1e-6

LANES = 128
HALF_ROPE = QK_ROPE // 2
QK_PAD = QK_NOPE + LANES
BF16_ROWS = 16
VMEM_LIMIT = 56 * 1024 * 1024


def _tile(dim, pref, align):
    if dim <= pref:
        return dim
    t = (pref // align) * align
    while t >= align:
        if dim % t == 0:
            return t
        t -= align
    return dim


def _params(sem):
    return pltpu.CompilerParams(dimension_semantics=sem, vmem_limit_bytes=VMEM_LIMIT)


def _layer_vec(p, l):
    depth, n = p.shape
    return pl.BlockSpec((None, 1, n), lambda *_: (l, 0, 0)), p.reshape(depth, 1, n)


def _rms(x, g):
    return x * lax.rsqrt(jnp.mean(x * x, axis=-1, keepdims=True) + EPS) * g


def _rope(x, cos, sg):
    return x * cos + pltpu.roll(x, 2 * HALF_ROPE, 1) * sg


def _mm_kernel(x_ref, w_ref, o_ref):
    o_ref[...] = jnp.dot(x_ref[...], w_ref[...], preferred_element_type=F32).astype(o_ref.dtype)


def _matmul(x, w, out_dtype, *, bm, bn, name, layer=None):
    m, kdim = x.shape
    n = w.shape[-1]
    bm, bn = _tile(m, bm, 8), _tile(n, bn, LANES)
    if layer is None:
        wspec = pl.BlockSpec((kdim, bn), lambda i, j: (0, j))
    else:
        wspec = pl.BlockSpec((None, kdim, bn), lambda i, j: (layer, 0, j))
    return pl.pallas_call(
        _mm_kernel,
        out_shape=jax.ShapeDtypeStruct((m, n), out_dtype),
        grid=(m // bm, n // bn),
        in_specs=[pl.BlockSpec((bm, kdim), lambda i, j: (i, 0)), wspec],
        out_specs=pl.BlockSpec((bm, bn), lambda i, j: (i, j)),
        compiler_params=_params(("parallel", "parallel")),
        name=name,
    )(x, w)


def _rmsnorm_kernel(x_ref, g_ref, h_ref):
    h_ref[...] = _rms(x_ref[...], g_ref[...]).astype(h_ref.dtype)


def _rmsnorm(x, g, l, *, br=256):
    t, d = x.shape
    br = _tile(t, br, 8)
    gspec, g3 = _layer_vec(g, l)
    return pl.pallas_call(
        _rmsnorm_kernel,
        out_shape=jax.ShapeDtypeStruct((t, d), BF16),
        grid=(t // br,),
        in_specs=[pl.BlockSpec((br, d), lambda i: (i, 0)), gspec],
        out_specs=pl.BlockSpec((br, d), lambda i: (i, 0)),
        compiler_params=_params(("parallel",)),
        name="rmsnorm",
    )(x, g3)


def _resnorm_kernel(x_ref, d_ref, gpost_ref, gnext_ref, xo_ref, h_ref):
    xn = x_ref[...] + _rms(d_ref[...].astype(F32), gpost_ref[...])
    xo_ref[...] = xn
    h_ref[...] = _rms(xn, gnext_ref[...]).astype(h_ref.dtype)


def _resadd_kernel(x_ref, d_ref, gpost_ref, xo_ref):
    xo_ref[...] = x_ref[...] + _rms(d_ref[...].astype(F32), gpost_ref[...])


def _resnorm(x, delta, g_post, l, g_next=None, l_next=None, *, br=128):
    t, d = x.shape
    br = _tile(t, br, 8)
    row = pl.BlockSpec((br, d), lambda i: (i, 0))
    pspec, gp3 = _layer_vec(g_post, l)
    if g_next is None:
        return pl.pallas_call(
            _resadd_kernel,
            out_shape=jax.ShapeDtypeStruct((t, d), F32),
            grid=(t // br,),
            in_specs=[row, row, pspec],
            out_specs=row,
            compiler_params=_params(("parallel",)),
            name="resadd",
        )(x, delta, gp3), None
    nspec, gn3 = _layer_vec(g_next, l_next)
    return pl.pallas_call(
        _resnorm_kernel,
        out_shape=(jax.ShapeDtypeStruct((t, d), F32), jax.ShapeDtypeStruct((t, d), BF16)),
        grid=(t // br,),
        in_specs=[row, row, pspec, nspec],
        out_specs=(row, row),
        compiler_params=_params(("parallel",)),
        name="resnorm",
    )(x, delta, gp3, gn3)


def _qproj_kernel(c_ref, g_ref, w_ref, cos_ref, sg_ref, o_ref, xn_ref, *, heads):
    @pl.when(pl.program_id(1) == 0)
    def _():
        xn = _rms(c_ref[...].astype(F32), g_ref[...]) * (ATTN_SCALE * LOG2E)
        xn_ref[...] = xn.astype(BF16)

    cos, sg = cos_ref[...], sg_ref[...]
    for h in range(heads):
        lo = h * QK_PAD
        acc = jnp.dot(xn_ref[...], w_ref[:, lo:lo + QK_PAD], preferred_element_type=F32)
        o_ref[:, lo:lo + QK_NOPE] = acc[:, :QK_NOPE].astype(o_ref.dtype)
        o_ref[:, lo + QK_NOPE:lo + QK_PAD] = _rope(acc[:, QK_NOPE:], cos, sg).astype(o_ref.dtype)


def _qproj(u_lat, g, w, l, cos4, sg4, *, seq, q_lora, bm=1024, heads=4):
    t = u_lat.shape[0]
    n = w.shape[-1]
    bm = _tile(seq, bm, BF16_ROWS)
    heads = min(heads, n // QK_PAD)
    bn = heads * QK_PAD
    spb = seq // bm
    tab = pl.BlockSpec((bm, LANES), lambda i, j: (i % spb, 0))
    gspec, g3 = _layer_vec(g, l)
    return pl.pallas_call(
        functools.partial(_qproj_kernel, heads=heads),
        out_shape=jax.ShapeDtypeStruct((t, n), BF16),
        grid=(t // bm, n // bn),
        in_specs=[pl.BlockSpec((bm, q_lora), lambda i, j: (i, 0)),
                  gspec,
                  pl.BlockSpec((None, q_lora, bn), lambda i, j: (l, 0, j)),
                  tab, tab],
        out_specs=pl.BlockSpec((bm, bn), lambda i, j: (i, j)),
        scratch_shapes=[pltpu.VMEM((bm, q_lora), BF16)],
        compiler_params=_params(("parallel", "arbitrary")),
        name="qproj",
    )(u_lat, g3, w, cos4, sg4)


def _kvproj_kernel(c_ref, kr_ref, g_ref, w_ref, cos_ref, sg_ref, k_ref, v_ref, krr_ref,
                   xn_ref, *, heads):
    @pl.when(pl.program_id(1) == 0)
    def _():
        xn_ref[...] = _rms(c_ref[...].astype(F32), g_ref[...]).astype(BF16)
        krr_ref[...] = _rope(kr_ref[...].astype(F32), cos_ref[...], sg_ref[...]).astype(BF16)

    kv = jnp.dot(xn_ref[...], w_ref[...], preferred_element_type=F32)
    width = QK_NOPE + V_HEAD
    for h in range(heads):
        k_ref[:, h * QK_NOPE:(h + 1) * QK_NOPE] = kv[:, h * width:h * width + QK_NOPE].astype(
            k_ref.dtype)
        v_ref[:, h * V_HEAD:(h + 1) * V_HEAD] = kv[:, h * width + QK_NOPE:(h + 1) * width].astype(
            v_ref.dtype)


def _kvproj(u_lat, g, w, l, cos4, sg4, *, seq, q_lora, kv_lora, bm=1024, heads=4):
    t = u_lat.shape[0]
    width = QK_NOPE + V_HEAD
    n_heads = w.shape[-1] // width
    bm = _tile(seq, bm, BF16_ROWS)
    heads = min(heads, n_heads)
    spb = seq // bm
    assert q_lora % kv_lora == 0 and (q_lora + kv_lora) % LANES == 0
    ckv_blk = q_lora // kv_lora
    kr_blk = (q_lora + kv_lora) // LANES
    tab = pl.BlockSpec((bm, LANES), lambda i, j: (i % spb, 0))
    gspec, g3 = _layer_vec(g, l)
    return pl.pallas_call(
        functools.partial(_kvproj_kernel, heads=heads),
        out_shape=(jax.ShapeDtypeStruct((t, n_heads * QK_NOPE), BF16),
                   jax.ShapeDtypeStruct((t, n_heads * V_HEAD), BF16),
                   jax.ShapeDtypeStruct((t, LANES), BF16)),
        grid=(t // bm, n_heads // heads),
        in_specs=[pl.BlockSpec((bm, kv_lora), lambda i, j: (i, ckv_blk)),
                  pl.BlockSpec((bm, LANES), lambda i, j: (i, kr_blk)),
                  gspec,
                  pl.BlockSpec((None, kv_lora, heads * width), lambda i, j: (l, 0, j)),
                  tab, tab],
        out_specs=(pl.BlockSpec((bm, heads * QK_NOPE), lambda i, j: (i, j)),
                   pl.BlockSpec((bm, heads * V_HEAD), lambda i, j: (i, j)),
                   pl.BlockSpec((bm, LANES), lambda i, j: (i, 0))),
        scratch_shapes=[pltpu.VMEM((bm, kv_lora), BF16)],
        compiler_params=_params(("parallel", "arbitrary")),
        name="kvproj",
    )(u_lat, u_lat, g3, w, cos4, sg4)


def _attn_kernel(q_ref, kn_ref, kr_ref, v_ref, o_ref, k_scr, vt_scr, qt_scr, sa_scr, sb_scr, *,
                 tk, nk, group):
    @pl.when(pl.program_id(2) == 0)
    def _():
        def build(j, c):
            rows = pl.ds(pl.multiple_of(j * tk, tk), tk)
            k_scr[rows, :QK_NOPE] = kn_ref[rows, :]
            k_scr[rows, QK_NOPE:] = kr_ref[rows, :]
            vt_scr[:, rows] = v_ref[rows, :].T
            return c
        lax.fori_loop(0, nk, build, 0)

    qt_scr[...] = q_ref[...].T
    tq = qt_scr.shape[1]

    def chunk(j):
        return pl.ds(j * tk if isinstance(j, int) else pl.multiple_of(j * tk, tk), tk)

    def scores(j, s_ref):
        s_ref[...] = jnp.dot(k_scr[chunk(j), :], qt_scr[...],
                             preferred_element_type=F32)

    def update(j, s_ref, carry):
        m, l, acc = carry
        m_new = jnp.maximum(m, jnp.max(s_ref[...], axis=0, keepdims=True))
        alpha = jnp.exp2(m - m_new)
        p = jnp.exp2(s_ref[...] - m_new)
        l = alpha * l + jnp.sum(p, axis=0, keepdims=True)
        acc = alpha * acc + jnp.dot(vt_scr[:, chunk(j)], p.astype(BF16),
                                    preferred_element_type=F32)
        return m_new, l, acc

    bufs = (sa_scr, sb_scr)

    def trip(i, carry):
        j0 = group * i
        for g in range(group):
            scores(j0 + g + 1, bufs[(g + 1) % 2])
            carry = update(j0 + g, bufs[g % 2], carry)
        return carry

    carry = (jnp.full((1, tq), -jnp.inf, F32), jnp.zeros((1, tq), F32),
             jnp.zeros((V_HEAD, tq), F32))
    scores(0, sa_scr)
    n_trips = (nk - 1) // group
    carry = lax.fori_loop(0, n_trips, trip, carry)
    for j in range(group * n_trips, nk):
        if j + 1 < nk:
            scores(j + 1, bufs[(j + 1) % 2])
        carry = update(j, bufs[j % 2], carry)
    _, l, acc = carry
    o_ref[...] = (acc / l).T.astype(o_ref.dtype)


def _attention(q, kn, krr, v, *, seq, tq=1024, tk=512, group=4):
    t = q.shape[0]
    n_heads = q.shape[1] // QK_PAD
    tq, tk = _tile(seq, tq, LANES), _tile(seq, tk, LANES)
    assert group % 2 == 0
    qpb = seq // tq
    return pl.pallas_call(
        functools.partial(_attn_kernel, tk=tk, nk=seq // tk, group=group),
        out_shape=jax.ShapeDtypeStruct((t, n_heads * V_HEAD), BF16),
        grid=(t // seq, n_heads, qpb),
        in_specs=[pl.BlockSpec((tq, QK_PAD), lambda b, h, i: (b * qpb + i, h)),
                  pl.BlockSpec((seq, QK_NOPE), lambda b, h, i: (b, h)),
                  pl.BlockSpec((seq, LANES), lambda b, h, i: (b, 0)),
                  pl.BlockSpec((seq, V_HEAD), lambda b, h, i: (b, h))],
        out_specs=pl.BlockSpec((tq, V_HEAD), lambda b, h, i: (b * qpb + i, h)),
        scratch_shapes=[pltpu.VMEM((seq, QK_PAD), BF16), pltpu.VMEM((V_HEAD, seq), BF16),
                        pltpu.VMEM((QK_PAD, tq), BF16),
                        pltpu.VMEM((tk, tq), F32), pltpu.VMEM((tk, tq), F32)],
        compiler_params=_params(("parallel", "parallel", "arbitrary")),
        name="attention",
    )(q, kn, krr, v)


def _conv_kernel(a_ref, g_ref, ap_ref, gp_ref, an_ref, gn_ref, w_ref, b_ref, lg_ref, lb_ref,
                 o_ref, h_ref, c_ref, *, ts, tiles_per_seq, width, rows):
    i = pl.program_id(0)
    halo = BF16_ROWS

    def glu(a, g):
        return a[...].astype(F32) * jax.nn.sigmoid(g[...].astype(F32))

    h_ref[halo:halo + ts, :] = glu(a_ref, g_ref)
    first = (i % tiles_per_seq) == 0
    last = (i % tiles_per_seq) == tiles_per_seq - 1
    h_ref[0:halo, :] = jnp.where(first, 0.0, glu(ap_ref, gp_ref))
    h_ref[halo + ts:2 * halo + ts, :] = jnp.where(last, 0.0, glu(an_ref, gn_ref))

    pad = (width - 1) // 2
    base = halo - pad
    span = ((base + width - 1) // 8) * 8
    wrows = rows + span + 8
    assert wrows <= rows + 2 * halo
    n_chunks = a_ref.shape[1] // LANES

    def lane_chunk(c, carry):
        lanes = pl.ds(pl.multiple_of(c * LANES, LANES), LANES)
        wts = [w_ref[pl.ds(k, 1), lanes] for k in range(width)]
        bias = b_ref[:, lanes]
        for r0 in range(0, ts, rows):
            aligned = h_ref[pl.ds(r0, wrows), lanes]
            acc = jnp.zeros((rows, LANES), F32)
            for r in range(8):
                taps = [k for k in range(width) if (base + k) % 8 == r]
                if not taps:
                    continue
                win = aligned if r == 0 else pltpu.roll(aligned, wrows - r, 0)
                for k in taps:
                    off = (base + k) - r
                    acc = acc + wts[k] * win[off:off + rows, :]
            c_ref[pl.ds(r0, rows), lanes] = acc + bias
        return carry

    lax.fori_loop(0, n_chunks, lane_chunk, 0)

    y = c_ref[...]
    mu = jnp.mean(y, axis=-1, keepdims=True)
    yc = y - mu
    var = jnp.mean(yc * yc, axis=-1, keepdims=True)
    z = yc * lax.rsqrt(var + EPS) * lg_ref[...] + lb_ref[...]
    o_ref[...] = (z * jax.nn.sigmoid(z)).astype(o_ref.dtype)


def _conv_branch(u_main, conv_w, conv_b, ln_g, ln_b, l, *, seq, d_conv, ts=256, rows=64):
    t = u_main.shape[0]
    width = conv_w.shape[1]
    halo = BF16_ROWS
    assert (width - 1) // 2 <= halo
    ts = _tile(seq, ts, halo)
    rows = _tile(ts, rows, 8)
    hb = ts // halo
    last_blk = t // halo - 1
    cur = lambda c: pl.BlockSpec((ts, d_conv), lambda i: (i, c))
    prev = lambda c: pl.BlockSpec((halo, d_conv), lambda i: (jnp.maximum(i * hb - 1, 0), c))
    nxt = lambda c: pl.BlockSpec((halo, d_conv), lambda i: (jnp.minimum((i + 1) * hb, last_blk), c))
    bspec, b3 = _layer_vec(conv_b, l)
    gspec, g3 = _layer_vec(ln_g, l)
    lspec, l3 = _layer_vec(ln_b, l)
    return pl.pallas_call(
        functools.partial(_conv_kernel, ts=ts, tiles_per_seq=seq // ts, width=width, rows=rows),
        out_shape=jax.ShapeDtypeStruct((t, d_conv), BF16),
        grid=(t // ts,),
        in_specs=[cur(0), cur(1), prev(0), prev(1), nxt(0), nxt(1),
                  pl.BlockSpec((None, width, d_conv), lambda i: (l, 0, 0)), bspec, gspec, lspec],
        out_specs=pl.BlockSpec((ts, d_conv), lambda i: (i, 0)),
        scratch_shapes=[pltpu.VMEM((ts + 2 * halo, d_conv), F32),
                        pltpu.VMEM((ts, d_conv), F32)],
        compiler_params=_params(("parallel",)),
        name="conv_branch",
    )(u_main, u_main, u_main, u_main, u_main, u_main, conv_w, b3, g3, l3)


def _merge_kernel(hc_ref, o_ref, wc_ref, wo_ref, gc_ref, ga_ref, out_ref):
    yc = jnp.dot(hc_ref[...], wc_ref[...], preferred_element_type=F32)
    ya = jnp.dot(o_ref[...], wo_ref[...], preferred_element_type=F32)
    gc = jax.nn.sigmoid(gc_ref[...].astype(F32))
    ga = jax.nn.sigmoid(ga_ref[...].astype(F32))
    out_ref[...] = (gc * yc + ga * ya).astype(out_ref.dtype)


def _merge(hc, o, wc, wo, l, u_main, *, d_conv, bm=1024, bn=512):
    t = hc.shape[0]
    d = wc.shape[-1]
    bm, bn = _tile(t, bm, BF16_ROWS), _tile(d, bn, LANES)
    assert (2 * d_conv) % bn == 0
    g0 = 2 * d_conv // bn
    nb = d // bn
    return pl.pallas_call(
        _merge_kernel,
        out_shape=jax.ShapeDtypeStruct((t, d), BF16),
        grid=(t // bm, nb),
        in_specs=[pl.BlockSpec((bm, hc.shape[1]), lambda i, j: (i, 0)),
                  pl.BlockSpec((bm, o.shape[1]), lambda i, j: (i, 0)),
                  pl.BlockSpec((None, wc.shape[1], bn), lambda i, j: (l, 0, j)),
                  pl.BlockSpec((None, wo.shape[1], bn), lambda i, j: (l, 0, j)),
                  pl.BlockSpec((bm, bn), lambda i, j: (i, g0 + j)),
                  pl.BlockSpec((bm, bn), lambda i, j: (i, g0 + nb + j))],
        out_specs=pl.BlockSpec((bm, bn), lambda i, j: (i, j)),
        compiler_params=_params(("parallel", "parallel")),
        name="merge",
    )(hc, o, wc, wo, u_main, u_main)


def _gateup_kernel(h_ref, wg_ref, wu_ref, o_ref):
    h = h_ref[...]
    gate = jnp.dot(h, wg_ref[...], preferred_element_type=F32)
    up = jnp.dot(h, wu_ref[...], preferred_element_type=F32)
    o_ref[...] = (gate * jax.nn.sigmoid(gate) * up).astype(o_ref.dtype)


def _gateup(h, wg, wu, l, *, bm=1024, bn=512):
    t, d = h.shape
    f = wg.shape[-1]
    bm, bn = _tile(t, bm, BF16_ROWS), min(bn, f)
    wspec = pl.BlockSpec((None, d, bn), lambda i, j: (l, 0, j))
    return pl.pallas_call(
        _gateup_kernel,
        out_shape=jax.ShapeDtypeStruct((t, f), BF16),
        grid=(t // bm, pl.cdiv(f, bn)),
        in_specs=[pl.BlockSpec((bm, d), lambda i, j: (i, 0)), wspec, wspec],
        out_specs=pl.BlockSpec((bm, bn), lambda i, j: (i, j)),
        compiler_params=_params(("parallel", "parallel")),
        name="gateup",
    )(h, wg, wu)


def _q_column_selector(n_heads):
    src = np.full((n_heads, QK_PAD), -1, np.int32)
    base = np.arange(n_heads)[:, None] * QK_HEAD
    src[:, :QK_NOPE + HALF_ROPE] = base + np.arange(QK_NOPE + HALF_ROPE)
    lo = QK_NOPE + 2 * HALF_ROPE
    src[:, lo:lo + HALF_ROPE] = base + QK_NOPE + HALF_ROPE + np.arange(HALF_ROPE)
    rows = lax.broadcasted_iota(jnp.int32, (n_heads * QK_HEAD, n_heads * QK_PAD), 0)
    return (rows == jnp.asarray(src.reshape(1, -1))).astype(BF16)


def _prep_weights(w_in, w_conv_out, w_uq, w_ukv, w_o_attn, w_out, w_gate, w_up, w_down,
                  *, d_conv, q_lora, kv_lora):
    depth, d, _ = w_in.shape
    off_cq = 2 * d_conv
    off_ckv = off_cq + q_lora
    off_kr = off_ckv + kv_lora
    off_gate = off_kr + QK_ROPE
    zr = jnp.zeros((depth, d, HALF_ROPE), w_in.dtype)
    w_main = jnp.concatenate([w_in[..., :off_cq], w_in[..., off_gate:]], axis=-1).astype(BF16)
    w_lat = jnp.concatenate([w_in[..., off_cq:off_kr],
                             w_in[..., off_kr:off_kr + HALF_ROPE], zr,
                             w_in[..., off_kr + HALF_ROPE:off_gate], zr], axis=-1).astype(BF16)

    n_heads = w_uq.shape[-1] // QK_HEAD
    wq = _matmul(w_uq.astype(BF16).reshape(depth * q_lora, n_heads * QK_HEAD),
                 _q_column_selector(n_heads), BF16, bm=512, bn=512, name="wq_layout").reshape(depth, q_lora, n_heads * QK_PAD)

    return dict(w_main=w_main, w_lat=w_lat, wq=wq, wkv=w_ukv.astype(BF16),
                wc=w_conv_out.astype(BF16), wo=w_o_attn.astype(BF16), w_out=w_out.astype(BF16),
                wg=w_gate.astype(BF16), wu=w_up.astype(BF16), w_dn=w_down.astype(BF16))


def _rope_tables(seq):
    inv_freq = ROPE_THETA ** (-jnp.arange(0, QK_ROPE, 2, dtype=F32) / QK_ROPE)
    ang = jnp.arange(seq, dtype=F32)[:, None] * inv_freq[None, :]
    cos, sin = jnp.cos(ang), jnp.sin(ang)
    return (jnp.concatenate([cos, cos, cos, cos], axis=-1),
            jnp.concatenate([-sin, -sin, sin, sin], axis=-1))


def _trunk(x3, pw, small, dims):
    b, seq, d = x3.shape
    x = x3.reshape(b * seq, d)
    depth = pw["w_main"].shape[0]
    d_conv, q_lora, kv_lora = dims
    cos4, sg4 = _rope_tables(seq)
    h = _rmsnorm(x, small["pre_norm_mix"], 0)
    for l in range(depth):
        u_main = _matmul(h, pw["w_main"], BF16, bm=1024, bn=1024, name="w_in_main", layer=l)
        u_lat = _matmul(h, pw["w_lat"], BF16, bm=512, bn=pw["w_lat"].shape[-1],
                        name="w_in_latent", layer=l)
        hc = _conv_branch(u_main, small["conv_w"], small["conv_b"], small["conv_ln_g"],
                          small["conv_ln_b"], l, seq=seq, d_conv=d_conv)
        q = _qproj(u_lat, small["q_norm"], pw["wq"], l, cos4, sg4, seq=seq, q_lora=q_lora)
        kn, v, krr = _kvproj(u_lat, small["kv_norm"], pw["wkv"], l, cos4, sg4,
                             seq=seq, q_lora=q_lora, kv_lora=kv_lora)
        o = _attention(q, kn, krr, v, seq=seq)
        mix = _merge(hc, o, pw["wc"], pw["wo"], l, u_main, d_conv=d_conv)
        mixed = _matmul(mix, pw["w_out"], BF16, bm=1024, bn=1024, name="w_out", layer=l)
        x, h = _resnorm(x, mixed, small["post_norm_mix"], l, small["pre_norm_ffn"], l)
        act = _gateup(h, pw["wg"], pw["wu"], l)
        f = _matmul(act, pw["w_dn"], BF16, bm=512, bn=512, name="w_down", layer=l)
        if l + 1 < depth:
            x, h = _resnorm(x, f, small["post_norm_ffn"], l, small["pre_norm_mix"], l + 1)
        else:
            x, _ = _resnorm(x, f, small["post_norm_ffn"], l)
    return x.reshape(b, seq, d)


def kernel(x_prompt, x_sample, pre_norm_mix, w_in, conv_w, conv_b, conv_ln_g, conv_ln_b, w_conv_out, q_norm, w_uq, kv_norm, w_ukv, w_o_attn, w_out, post_norm_mix, pre_norm_ffn, w_gate, w_up, w_down, post_norm_ffn):
    d_conv = conv_w.shape[-1]
    q_lora = q_norm.shape[-1]
    kv_lora = kv_norm.shape[-1]
    pw = _prep_weights(w_in, w_conv_out, w_uq, w_ukv, w_o_attn, w_out, w_gate, w_up, w_down,
                       d_conv=d_conv, q_lora=q_lora, kv_lora=kv_lora)
    small = dict(pre_norm_mix=pre_norm_mix, conv_w=conv_w, conv_b=conv_b, conv_ln_g=conv_ln_g,
                 conv_ln_b=conv_ln_b, q_norm=q_norm, kv_norm=kv_norm, post_norm_mix=post_norm_mix,
                 pre_norm_ffn=pre_norm_ffn, post_norm_ffn=post_norm_ffn)
    dims = (d_conv, q_lora, kv_lora)
    return (_trunk(x_prompt, pw, small, dims), _trunk(x_sample, pw, small, dims))
```

```python
import functools

import numpy as np
import jax
import jax.numpy as jnp
from jax import lax
from jax.experimental import pallas as pl
from jax.experimental.pallas import tpu as pltpu

F32 = jnp.float32
BF16 = jnp.bfloat16

QK_NOPE = 128
QK_ROPE = 64
QK_HEAD = QK_NOPE + QK_ROPE
V_HEAD = 128
ROPE_THETA = 10000.0
ATTN_SCALE = QK_HEAD ** -0.5
LOG2E = 1.4426950408889634
EPS = 1e-6

LANES = 128
HALF_ROPE = QK_ROPE // 2
QK_PAD = QK_NOPE + LANES
BF16_ROWS = 16
VMEM_LIMIT = 56 * 1024 * 1024


def _tile(dim, pref, align):
    if dim <= pref:
        return dim
    t = (pref // align) * align
    while t >= align:
        if dim % t == 0:
            return t
        t -= align
    return dim


def _params(sem):
    return pltpu.CompilerParams(dimension_semantics=sem, vmem_limit_bytes=VMEM_LIMIT)


def _layer_vec(p, l):
    depth, n = p.shape
    return pl.BlockSpec((None, 1, n), lambda *_: (l, 0, 0)), p.reshape(depth, 1, n)


def _rms(x, g):
    return x * lax.rsqrt(jnp.mean(x * x, axis=-1, keepdims=True) + EPS) * g


def _rope(x, cos, sg):
    return x * cos + pltpu.roll(x, 2 * HALF_ROPE, 1) * sg


def _mm_kernel(x_ref, w_ref, o_ref):
    o_ref[...] = jnp.dot(x_ref[...], w_ref[...], preferred_element_type=F32).astype(o_ref.dtype)


def _matmul(x, w, out_dtype, *, bm, bn, name, layer=None):
    m, kdim = x.shape
    n = w.shape[-1]
    bm, bn = _tile(m, bm, 8), min(bn, n)
    if layer is None:
        wspec = pl.BlockSpec((kdim, bn), lambda i, j: (0, j))
    else:
        wspec = pl.BlockSpec((None, kdim, bn), lambda i, j: (layer, 0, j))
    return pl.pallas_call(
        _mm_kernel,
        out_shape=jax.ShapeDtypeStruct((m, n), out_dtype),
        grid=(m // bm, pl.cdiv(n, bn)),
        in_specs=[pl.BlockSpec((bm, kdim), lambda i, j: (i, 0)), wspec],
        out_specs=pl.BlockSpec((bm, bn), lambda i, j: (i, j)),
        compiler_params=_params(("parallel", "parallel")),
        name=name,
    )(x, w)


def _rmsnorm_kernel(x_ref, g_ref, h_ref):
    h_ref[...] = _rms(x_ref[...], g_ref[...]).astype(h_ref.dtype)


def _rmsnorm(x, g, l, *, br=256):
    t, d = x.shape
    br = _tile(t, br, 8)
    gspec, g3 = _layer_vec(g, l)
    return pl.pallas_call(
        _rmsnorm_kernel,
        out_shape=jax.ShapeDtypeStruct((t, d), BF16),
        grid=(t // br,),
        in_specs=[pl.BlockSpec((br, d), lambda i: (i, 0)), gspec],
        out_specs=pl.BlockSpec((br, d), lambda i: (i, 0)),
        compiler_params=_params(("parallel",)),
        name="rmsnorm",
    )(x, g3)


def _resnorm_kernel(x_ref, d_ref, gpost_ref, gnext_ref, xo_ref, h_ref):
    xn = x_ref[...] + _rms(d_ref[...].astype(F32), gpost_ref[...])
    xo_ref[...] = xn
    h_ref[...] = _rms(xn, gnext_ref[...]).astype(h_ref.dtype)


def _resadd_kernel(x_ref, d_ref, gpost_ref, xo_ref):
    xo_ref[...] = x_ref[...] + _rms(d_ref[...].astype(F32), gpost_ref[...])


def _resnorm(x, delta, g_post, l, g_next=None, l_next=None, *, br=256):
    t, d = x.shape
    br = _tile(t, br, 8)
    row = pl.BlockSpec((br, d), lambda i: (i, 0))
    pspec, gp3 = _layer_vec(g_post, l)
    if g_next is None:
        return pl.pallas_call(
            _resadd_kernel,
            out_shape=jax.ShapeDtypeStruct((t, d), F32),
            grid=(t // br,),
            in_specs=[row, row, pspec],
            out_specs=row,
            compiler_params=_params(("parallel",)),
            name="resadd",
        )(x, delta, gp3), None
    nspec, gn3 = _layer_vec(g_next, l_next)
    return pl.pallas_call(
        _resnorm_kernel,
        out_shape=(jax.ShapeDtypeStruct((t, d), F32), jax.ShapeDtypeStruct((t, d), BF16)),
        grid=(t // br,),
        in_specs=[row, row, pspec, nspec],
        out_specs=(row, row),
        compiler_params=_params(("parallel",)),
        name="resnorm",
    )(x, delta, gp3, gn3)


def _qproj_kernel(c_ref, g_ref, w_ref, cos_ref, sg_ref, o_ref, xn_ref, *, heads):
    @pl.when(pl.program_id(1) == 0)
    def _():
        xn = _rms(c_ref[...].astype(F32), g_ref[...]) * (ATTN_SCALE * LOG2E)
        xn_ref[...] = xn.astype(BF16)

    acc = jnp.dot(xn_ref[...], w_ref[...], preferred_element_type=F32)
    cos, sg = cos_ref[...], sg_ref[...]
    for h in range(heads):
        lo = h * QK_PAD
        o_ref[:, lo:lo + QK_NOPE] = acc[:, lo:lo + QK_NOPE].astype(o_ref.dtype)
        o_ref[:, lo + QK_NOPE:lo + QK_PAD] = _rope(
            acc[:, lo + QK_NOPE:lo + QK_PAD], cos, sg).astype(o_ref.dtype)


def _qproj(u, g, w, l, cos4, sg4, *, seq, lat0, q_lora, bm=1024, heads=8):
    t = u.shape[0]
    n = w.shape[-1]
    bm = _tile(seq, bm, BF16_ROWS)
    heads = min(heads, n // QK_PAD)
    bn = heads * QK_PAD
    spb = seq // bm
    assert lat0 % q_lora == 0
    cq_blk = lat0 // q_lora
    tab = pl.BlockSpec((bm, LANES), lambda i, j: (i % spb, 0))
    gspec, g3 = _layer_vec(g, l)
    return pl.pallas_call(
        functools.partial(_qproj_kernel, heads=heads),
        out_shape=jax.ShapeDtypeStruct((t, n), BF16),
        grid=(t // bm, n // bn),
        in_specs=[pl.BlockSpec((bm, q_lora), lambda i, j: (i, cq_blk)),
                  gspec,
                  pl.BlockSpec((None, q_lora, bn), lambda i, j: (l, 0, j)),
                  tab, tab],
        out_specs=pl.BlockSpec((bm, bn), lambda i, j: (i, j)),
        scratch_shapes=[pltpu.VMEM((bm, q_lora), BF16)],
        compiler_params=_params(("parallel", "arbitrary")),
        name="qproj",
    )(u, g3, w, cos4, sg4)


def _kvproj_kernel(c_ref, kr_ref, g_ref, w_ref, cos_ref, sg_ref, k_ref, v_ref, krr_ref,
                   xn_ref, *, heads):
    @pl.when(pl.program_id(1) == 0)
    def _():
        xn_ref[...] = _rms(c_ref[...].astype(F32), g_ref[...]).astype(BF16)
        krr_ref[...] = _rope(kr_ref[...].astype(F32), cos_ref[...], sg_ref[...]).astype(BF16)

    kv = jnp.dot(xn_ref[...], w_ref[...], preferred_element_type=F32)
    width = QK_NOPE + V_HEAD
    for h in range(heads):
        k_ref[:, h * QK_NOPE:(h + 1) * QK_NOPE] = kv[:, h * width:h * width + QK_NOPE].astype(
            k_ref.dtype)
        v_ref[:, h * V_HEAD:(h + 1) * V_HEAD] = kv[:, h * width + QK_NOPE:(h + 1) * width].astype(
            v_ref.dtype)


def _kvproj(u, g, w, l, cos4, sg4, *, seq, lat0, q_lora, kv_lora, bm=1024, heads=8):
    t = u.shape[0]
    width = QK_NOPE + V_HEAD
    n_heads = w.shape[-1] // width
    bm = _tile(seq, bm, BF16_ROWS)
    heads = min(heads, n_heads)
    spb = seq // bm
    ckv0 = lat0 + q_lora
    assert ckv0 % kv_lora == 0 and (ckv0 + kv_lora) % LANES == 0
    ckv_blk = ckv0 // kv_lora
    kr_blk = (ckv0 + kv_lora) // LANES
    tab = pl.BlockSpec((bm, LANES), lambda i, j: (i % spb, 0))
    gspec, g3 = _layer_vec(g, l)
    return pl.pallas_call(
        functools.partial(_kvproj_kernel, heads=heads),
        out_shape=(jax.ShapeDtypeStruct((t, n_heads * QK_NOPE), BF16),
                   jax.ShapeDtypeStruct((t, n_heads * V_HEAD), BF16),
                   jax.ShapeDtypeStruct((t, LANES), BF16)),
        grid=(t // bm, n_heads // heads),
        in_specs=[pl.BlockSpec((bm, kv_lora), lambda i, j: (i, ckv_blk)),
                  pl.BlockSpec((bm, LANES), lambda i, j: (i, kr_blk)),
                  gspec,
                  pl.BlockSpec((None, kv_lora, heads * width), lambda i, j: (l, 0, j)),
                  tab, tab],
        out_specs=(pl.BlockSpec((bm, heads * QK_NOPE), lambda i, j: (i, j)),
                   pl.BlockSpec((bm, heads * V_HEAD), lambda i, j: (i, j)),
                   pl.BlockSpec((bm, LANES), lambda i, j: (i, 0))),
        scratch_shapes=[pltpu.VMEM((bm, kv_lora), BF16)],
        compiler_params=_params(("parallel", "arbitrary")),
        name="kvproj",
    )(u, u, g3, w, cos4, sg4)


def _attn_kernel(q_ref, kn_ref, kr_ref, v_ref, o_ref, k_scr, vt_scr, qt_scr, sa_scr, sb_scr, *,
                 tk, nk, group):
    @pl.when(pl.program_id(2) == 0)
    def _():
        def build(j, c):
            rows = pl.ds(pl.multiple_of(j * tk, tk), tk)
            k_scr[rows, :QK_NOPE] = kn_ref[rows, :]
            k_scr[rows, QK_NOPE:] = kr_ref[rows, :]
            vt_scr[:, rows] = v_ref[rows, :].T
            return c
        lax.fori_loop(0, nk, build, 0)

    qt_scr[...] = q_ref[...].T
    tq = qt_scr.shape[1]

    def chunk(j):
        return pl.ds(j * tk if isinstance(j, int) else pl.multiple_of(j * tk, tk), tk)

    def scores(j, s_ref):
        s_ref[...] = jnp.dot(k_scr[chunk(j), :], qt_scr[...],
                             preferred_element_type=F32)

    def update(j, s_ref, carry):
        m, l, acc = carry
        m_new = jnp.maximum(m, jnp.max(s_ref[...], axis=0, keepdims=True))
        alpha = jnp.exp2(m - m_new)
        p = jnp.exp2(s_ref[...] - m_new)
        l = alpha * l + jnp.sum(p, axis=0, keepdims=True)
        acc = alpha * acc + jnp.dot(vt_scr[:, chunk(j)], p.astype(BF16),
                                    preferred_element_type=F32)
        return m_new, l, acc

    bufs = (sa_scr, sb_scr)

    def trip(i, carry):
        j0 = group * i
        for g in range(group):
            scores(j0 + g + 1, bufs[(g + 1) % 2])
            carry = update(j0 + g, bufs[g % 2], carry)
        return carry

    carry = (jnp.full((1, tq), -jnp.inf, F32), jnp.zeros((1, tq), F32),
             jnp.zeros((V_HEAD, tq), F32))
    scores(0, sa_scr)
    n_trips = (nk - 1) // group
    carry = lax.fori_loop(0, n_trips, trip, carry)
    for j in range(group * n_trips, nk):
        if j + 1 < nk:
            scores(j + 1, bufs[(j + 1) % 2])
        carry = update(j, bufs[j % 2], carry)
    _, l, acc = carry
    o_ref[...] = (acc / l).T.astype(o_ref.dtype)


def _attention(q, kn, krr, v, *, seq, tq=1024, tq_whole=2048, tk=512, group=4):
    t = q.shape[0]
    n_heads = q.shape[1] // QK_PAD
    tq = seq if seq <= tq_whole else _tile(seq, tq, LANES)
    tk = _tile(seq, tk, LANES)
    assert group % 2 == 0
    qpb = seq // tq
    return pl.pallas_call(
        functools.partial(_attn_kernel, tk=tk, nk=seq // tk, group=group),
        out_shape=jax.ShapeDtypeStruct((t, n_heads * V_HEAD), BF16),
        grid=(t // seq, n_heads, qpb),
        in_specs=[pl.BlockSpec((tq, QK_PAD), lambda b, h, i: (b * qpb + i, h)),
                  pl.BlockSpec((seq, QK_NOPE), lambda b, h, i: (b, h)),
                  pl.BlockSpec((seq, LANES), lambda b, h, i: (b, 0)),
                  pl.BlockSpec((seq, V_HEAD), lambda b, h, i: (b, h))],
        out_specs=pl.BlockSpec((tq, V_HEAD), lambda b, h, i: (b * qpb + i, h)),
        scratch_shapes=[pltpu.VMEM((seq, QK_PAD), BF16), pltpu.VMEM((V_HEAD, seq), BF16),
                        pltpu.VMEM((QK_PAD, tq), BF16),
                        pltpu.VMEM((tk, tq), F32), pltpu.VMEM((tk, tq), F32)],
        compiler_params=_params(("parallel", "parallel", "arbitrary")),
        name="attention",
    )(q, kn, krr, v)


def _conv_kernel(a_ref, g_ref, ap_ref, gp_ref, an_ref, gn_ref, w_ref, b_ref, lg_ref, lb_ref,
                 o_ref, h_ref, c_ref, *, ts, tiles_per_seq, width, rows):
    i = pl.program_id(0)
    halo = BF16_ROWS

    def glu(a, g):
        return a[...].astype(F32) * jax.nn.sigmoid(g[...].astype(F32))

    h_ref[halo:halo + ts, :] = glu(a_ref, g_ref)
    first = (i % tiles_per_seq) == 0
    last = (i % tiles_per_seq) == tiles_per_seq - 1
    h_ref[0:halo, :] = jnp.where(first, 0.0, glu(ap_ref, gp_ref))
    h_ref[halo + ts:2 * halo + ts, :] = jnp.where(last, 0.0, glu(an_ref, gn_ref))

    pad = (width - 1) // 2
    base = halo - pad
    span = ((base + width - 1) // 8) * 8
    wrows = rows + span + 8
    assert wrows <= rows + 2 * halo
    n_chunks = a_ref.shape[1] // LANES

    def lane_chunk(c, carry):
        lanes = pl.ds(pl.multiple_of(c * LANES, LANES), LANES)
        wts = [w_ref[pl.ds(k, 1), lanes] for k in range(width)]
        bias = b_ref[:, lanes]
        for r0 in range(0, ts, rows):
            aligned = h_ref[pl.ds(r0, wrows), lanes]
            acc = jnp.zeros((rows, LANES), F32)
            for r in range(8):
                taps = [k for k in range(width) if (base + k) % 8 == r]
                if not taps:
                    continue
                win = aligned if r == 0 else pltpu.roll(aligned, wrows - r, 0)
                for k in taps:
                    off = (base + k) - r
                    acc = acc + wts[k] * win[off:off + rows, :]
            c_ref[pl.ds(r0, rows), lanes] = acc + bias
        return carry

    lax.fori_loop(0, n_chunks, lane_chunk, 0)

    y = c_ref[...]
    mu = jnp.mean(y, axis=-1, keepdims=True)
    yc = y - mu
    var = jnp.mean(yc * yc, axis=-1, keepdims=True)
    z = yc * lax.rsqrt(var + EPS) * lg_ref[...] + lb_ref[...]
    o_ref[...] = (z * jax.nn.sigmoid(z)).astype(o_ref.dtype)


def _conv_branch(u_main, conv_w, conv_b, ln_g, ln_b, l, *, seq, d_conv, ts=256, rows=64):
    t = u_main.shape[0]
    width = conv_w.shape[1]
    halo = BF16_ROWS
    assert (width - 1) // 2 <= halo
    ts = _tile(seq, ts, halo)
    rows = _tile(ts, rows, 8)
    hb = ts // halo
    last_blk = t // halo - 1
    cur = lambda c: pl.BlockSpec((ts, d_conv), lambda i: (i, c))
    prev = lambda c: pl.BlockSpec((halo, d_conv), lambda i: (jnp.maximum(i * hb - 1, 0), c))
    nxt = lambda c: pl.BlockSpec((halo, d_conv), lambda i: (jnp.minimum((i + 1) * hb, last_blk), c))
    bspec, b3 = _layer_vec(conv_b, l)
    gspec, g3 = _layer_vec(ln_g, l)
    lspec, l3 = _layer_vec(ln_b, l)
    return pl.pallas_call(
        functools.partial(_conv_kernel, ts=ts, tiles_per_seq=seq // ts, width=width, rows=rows),
        out_shape=jax.ShapeDtypeStruct((t, d_conv), BF16),
        grid=(t // ts,),
        in_specs=[cur(0), cur(1), prev(0), prev(1), nxt(0), nxt(1),
                  pl.BlockSpec((None, width, d_conv), lambda i: (l, 0, 0)), bspec, gspec, lspec],
        out_specs=pl.BlockSpec((ts, d_conv), lambda i: (i, 0)),
        scratch_shapes=[pltpu.VMEM((ts + 2 * halo, d_conv), F32),
                        pltpu.VMEM((ts, d_conv), F32)],
        compiler_params=_params(("parallel",)),
        name="conv_branch",
    )(u_main, u_main, u_main, u_main, u_main, u_main, conv_w, b3, g3, l3)


def _merge_kernel(hc_ref, o_ref, wc_ref, wo_ref, gc_ref, ga_ref, out_ref):
    yc = jnp.dot(hc_ref[...], wc_ref[...], preferred_element_type=F32)
    ya = jnp.dot(o_ref[...], wo_ref[...], preferred_element_type=F32)
    gc = jax.nn.sigmoid(gc_ref[...].astype(F32))
    ga = jax.nn.sigmoid(ga_ref[...].astype(F32))
    out_ref[...] = (gc * yc + ga * ya).astype(out_ref.dtype)


def _merge(hc, o, wc, wo, l, u_main, *, d_conv, bm=1024, bn=512):
    t = hc.shape[0]
    d = wc.shape[-1]
    bm, bn = _tile(t, bm, BF16_ROWS), _tile(d, bn, LANES)
    assert (2 * d_conv) % bn == 0
    g0 = 2 * d_conv // bn
    nb = d // bn
    return pl.pallas_call(
        _merge_kernel,
        out_shape=jax.ShapeDtypeStruct((t, d), BF16),
        grid=(t // bm, nb),
        in_specs=[pl.BlockSpec((bm, hc.shape[1]), lambda i, j: (i, 0)),
                  pl.BlockSpec((bm, o.shape[1]), lambda i, j: (i, 0)),
                  pl.BlockSpec((None, wc.shape[1], bn), lambda i, j: (l, 0, j)),
                  pl.BlockSpec((None, wo.shape[1], bn), lambda i, j: (l, 0, j)),
                  pl.BlockSpec((bm, bn), lambda i, j: (i, g0 + j)),
                  pl.BlockSpec((bm, bn), lambda i, j: (i, g0 + nb + j))],
        out_specs=pl.BlockSpec((bm, bn), lambda i, j: (i, j)),
        compiler_params=_params(("parallel", "parallel")),
        name="merge",
    )(hc, o, wc, wo, u_main, u_main)


def _gateup_kernel(h_ref, wg_ref, wu_ref, o_ref):
    h = h_ref[...]
    gate = jnp.dot(h, wg_ref[...], preferred_element_type=F32)
    up = jnp.dot(h, wu_ref[...], preferred_element_type=F32)
    o_ref[...] = (gate * jax.nn.sigmoid(gate) * up).astype(o_ref.dtype)


def _gateup(h, wg, wu, l, *, bm=1024, bn=512):
    t, d = h.shape
    f = wg.shape[-1]
    bm, bn = _tile(t, bm, BF16_ROWS), min(bn, f)
    wspec = pl.BlockSpec((None, d, bn), lambda i, j: (l, 0, j))
    return pl.pallas_call(
        _gateup_kernel,
        out_shape=jax.ShapeDtypeStruct((t, f), BF16),
        grid=(t // bm, pl.cdiv(f, bn)),
        in_specs=[pl.BlockSpec((bm, d), lambda i, j: (i, 0)), wspec, wspec],
        out_specs=pl.BlockSpec((bm, bn), lambda i, j: (i, j)),
        compiler_params=_params(("parallel", "parallel")),
        name="gateup",
    )(h, wg, wu)


def _q_column_selector(n_heads):
    src = np.full((n_heads, QK_PAD), -1, np.int32)
    base = np.arange(n_heads)[:, None] * QK_HEAD
    src[:, :QK_NOPE + HALF_ROPE] = base + np.arange(QK_NOPE + HALF_ROPE)
    lo = QK_NOPE + 2 * HALF_ROPE
    src[:, lo:lo + HALF_ROPE] = base + QK_NOPE + HALF_ROPE + np.arange(HALF_ROPE)
    rows = lax.broadcasted_iota(jnp.int32, (n_heads * QK_HEAD, n_heads * QK_PAD), 0)
    return (rows == jnp.asarray(src.reshape(1, -1))).astype(BF16)


def _prep_weights(w_in, w_conv_out, w_uq, w_ukv, w_o_attn, w_out, w_gate, w_up, w_down,
                  *, d_conv, q_lora, kv_lora):
    depth, d, _ = w_in.shape
    off_cq = 2 * d_conv
    off_ckv = off_cq + q_lora
    off_kr = off_ckv + kv_lora
    off_gate = off_kr + QK_ROPE
    zr = jnp.zeros((depth, d, HALF_ROPE), w_in.dtype)
    w_main = jnp.concatenate([w_in[..., :off_cq], w_in[..., off_gate:]], axis=-1).astype(BF16)
    w_lat = jnp.concatenate([w_in[..., off_cq:off_kr],
                             w_in[..., off_kr:off_kr + HALF_ROPE], zr,
                             w_in[..., off_kr + HALF_ROPE:off_gate], zr], axis=-1).astype(BF16)

    n_heads = w_uq.shape[-1] // QK_HEAD
    wq = _matmul(w_uq.astype(BF16).reshape(depth * q_lora, n_heads * QK_HEAD),
                 _q_column_selector(n_heads), BF16, bm=512, bn=512,
                 name="wq_layout").reshape(depth, q_lora, n_heads * QK_PAD)

    return dict(w_main=w_main, w_lat=w_lat, wq=wq, wkv=w_ukv.astype(BF16),
                wc=w_conv_out.astype(BF16), wo=w_o_attn.astype(BF16), w_out=w_out.astype(BF16),
                wg=w_gate.astype(BF16), wu=w_up.astype(BF16), w_dn=w_down.astype(BF16))


def _rope_tables(seq):
    inv_freq = ROPE_THETA ** (-jnp.arange(0, QK_ROPE, 2, dtype=F32) / QK_ROPE)
    ang = jnp.arange(seq, dtype=F32)[:, None] * inv_freq[None, :]
    cos, sin = jnp.cos(ang), jnp.sin(ang)
    return (jnp.concatenate([cos, cos, cos, cos], axis=-1),
            jnp.concatenate([-sin, -sin, sin, sin], axis=-1))


def _trunk(x3, pw, small, dims):
    b, seq, d = x3.shape
    x = x3.reshape(b * seq, d)
    depth = pw["w_main"].shape[0]
    d_conv, q_lora, kv_lora = dims
    cos4, sg4 = _rope_tables(seq)
    h = _rmsnorm(x, small["pre_norm_mix"], 0)
    for l in range(depth):
        u_main = _matmul(h, pw["w_main"], BF16, bm=1024, bn=1024, name="w_in_main", layer=l)
        u_lat = _matmul(h, pw["w_lat"], BF16, bm=512, bn=pw["w_lat"].shape[-1],
                        name="w_in_latent", layer=l)
        hc = _conv_branch(u_main, small["conv_w"], small["conv_b"], small["conv_ln_g"],
                          small["conv_ln_b"], l, seq=seq, d_conv=d_conv)
        q = _qproj(u_lat, small["q_norm"], pw["wq"], l, cos4, sg4, seq=seq, lat0=0,
                   q_lora=q_lora)
        kn, v, krr = _kvproj(u_lat, small["kv_norm"], pw["wkv"], l, cos4, sg4,
                             seq=seq, lat0=0, q_lora=q_lora, kv_lora=kv_lora)
        o = _attention(q, kn, krr, v, seq=seq)
        mix = _merge(hc, o, pw["wc"], pw["wo"], l, u_main, d_conv=d_conv)
        mixed = _matmul(mix, pw["w_out"], BF16, bm=1024, bn=1024, name="w_out", layer=l)
        x, h = _resnorm(x, mixed, small["post_norm_mix"], l, small["pre_norm_ffn"], l)
        act = _gateup(h, pw["wg"], pw["wu"], l)
        f = _matmul(act, pw["w_dn"], BF16, bm=512, bn=512, name="w_down", layer=l)
        if l + 1 < depth:
            x, h = _resnorm(x, f, small["post_norm_ffn"], l, small["pre_norm_mix"], l + 1)
        else:
            x, _ = _resnorm(x, f, small["post_norm_ffn"], l)
    return x.reshape(b, seq, d)


def kernel(x_prompt, x_sample, pre_norm_mix, w_in, conv_w, conv_b, conv_ln_g, conv_ln_b, w_conv_out, q_norm, w_uq, kv_norm, w_ukv, w_o_attn, w_out, post_norm_mix, pre_norm_ffn, w_gate, w_up, w_down, post_norm_ffn):
    d_conv = conv_w.shape[-1]
    q_lora = q_norm.shape[-1]
    kv_lora = kv_norm.shape[-1]
    pw = _prep_weights(w_in, w_conv_out, w_uq, w_ukv, w_o_attn, w_out, w_gate, w_up, w_down,
                       d_conv=d_conv, q_lora=q_lora, kv_lora=kv_lora)
    small = dict(pre_norm_mix=pre_norm_mix, conv_w=conv_w, conv_b=conv_b, conv_ln_g=conv_ln_g,
                 conv_ln_b=conv_ln_b, q_norm=q_norm, kv_norm=kv_norm, post_norm_mix=post_norm_mix,
                 pre_norm_ffn=pre_norm_ffn, post_norm_ffn=post_norm_ffn)
    dims = (d_conv, q_lora, kv_lora)
    return (_trunk(x_prompt, pw, small, dims), _trunk(x_sample, pw, small, dims))
```

```python
import functools

import numpy as np
import jax
import jax.numpy as jnp
from jax import lax
from jax.experimental import pallas as pl
from jax.experimental.pallas import tpu as pltpu

F32 = jnp.float32
BF16 = jnp.bfloat16

QK_NOPE = 128
QK_ROPE = 64
QK_HEAD = QK_NOPE + QK_ROPE
V_HEAD = 128
ROPE_THETA = 10000.0
ATTN_SCALE = QK_HEAD ** -0.5
LOG2E = 1.4426950408889634
EPS = 1e-6

LANES = 128
HALF_ROPE = QK_ROPE // 2
QK_PAD = QK_NOPE + LANES
BF16_ROWS = 16
VMEM_LIMIT = 56 * 1024 * 1024


def _tile(dim, pref, align):
    if dim <= pref:
        return dim
    t = (pref // align) * align
    while t >= align:
        if dim % t == 0:
            return t
        t -= align
    return dim


def _params(sem):
    return pltpu.CompilerParams(dimension_semantics=sem, vmem_limit_bytes=VMEM_LIMIT)


def _layer_vec(p, l):
    depth, n = p.shape
    return pl.BlockSpec((None, 1, n), lambda *_: (l, 0, 0)), p.reshape(depth, 1, n)


def _rms(x, g):
    return x * lax.rsqrt(jnp.mean(x * x, axis=-1, keepdims=True) + EPS) * g


def _rope(x, cos, sg):
    return x * cos + pltpu.roll(x, 2 * HALF_ROPE, 1) * sg


def _mm_kernel(x_ref, w_ref, o_ref):
    o_ref[...] = jnp.dot(x_ref[...], w_ref[...], preferred_element_type=F32).astype(o_ref.dtype)


def _matmul(x, w, out_dtype, *, bm, bn, name, layer=None):
    m, kdim = x.shape
    n = w.shape[-1]
    bm, bn = _tile(m, bm, 8), min(bn, n)
    if layer is None:
        wspec = pl.BlockSpec((kdim, bn), lambda i, j: (0, j))
    else:
        wspec = pl.BlockSpec((None, kdim, bn), lambda i, j: (layer, 0, j))
    return pl.pallas_call(
        _mm_kernel,
        out_shape=jax.ShapeDtypeStruct((m, n), out_dtype),
        grid=(m // bm, pl.cdiv(n, bn)),
        in_specs=[pl.BlockSpec((bm, kdim), lambda i, j: (i, 0)), wspec],
        out_specs=pl.BlockSpec((bm, bn), lambda i, j: (i, j)),
        compiler_params=_params(("parallel", "parallel")),
        name=name,
    )(x, w)


def _rmsnorm_kernel(x_ref, g_ref, h_ref):
    h_ref[...] = _rms(x_ref[...], g_ref[...]).astype(h_ref.dtype)


def _rmsnorm(x, g, l, *, br=256):
    t, d = x.shape
    br = _tile(t, br, 8)
    gspec, g3 = _layer_vec(g, l)
    return pl.pallas_call(
        _rmsnorm_kernel,
        out_shape=jax.ShapeDtypeStruct((t, d), BF16),
        grid=(t // br,),
        in_specs=[pl.BlockSpec((br, d), lambda i: (i, 0)), gspec],
        out_specs=pl.BlockSpec((br, d), lambda i: (i, 0)),
        compiler_params=_params(("parallel",)),
        name="rmsnorm",
    )(x, g3)


def _resnorm_kernel(x_ref, d_ref, gpost_ref, gnext_ref, xo_ref, h_ref):
    xn = x_ref[...] + _rms(d_ref[...].astype(F32), gpost_ref[...])
    xo_ref[...] = xn
    h_ref[...] = _rms(xn, gnext_ref[...]).astype(h_ref.dtype)


def _resadd_kernel(x_ref, d_ref, gpost_ref, xo_ref):
    xo_ref[...] = x_ref[...] + _rms(d_ref[...].astype(F32), gpost_ref[...])


def _resnorm(x, delta, g_post, l, g_next=None, l_next=None, *, br=256):
    t, d = x.shape
    br = _tile(t, br, 8)
    row = pl.BlockSpec((br, d), lambda i: (i, 0))
    pspec, gp3 = _layer_vec(g_post, l)
    if g_next is None:
        return pl.pallas_call(
            _resadd_kernel,
            out_shape=jax.ShapeDtypeStruct((t, d), F32),
            grid=(t // br,),
            in_specs=[row, row, pspec],
            out_specs=row,
            compiler_params=_params(("parallel",)),
            name="resadd",
        )(x, delta, gp3), None
    nspec, gn3 = _layer_vec(g_next, l_next)
    return pl.pallas_call(
        _resnorm_kernel,
        out_shape=(jax.ShapeDtypeStruct((t, d), F32), jax.ShapeDtypeStruct((t, d), BF16)),
        grid=(t // br,),
        in_specs=[row, row, pspec, nspec],
        out_specs=(row, row),
        compiler_params=_params(("parallel",)),
        name="resnorm",
    )(x, delta, gp3, gn3)


def _qproj_kernel(c_ref, g_ref, w_ref, cos_ref, sg_ref, o_ref, xn_ref, *, heads):
    @pl.when(pl.program_id(1) == 0)
    def _():
        xn = _rms(c_ref[...].astype(F32), g_ref[...]) * (ATTN_SCALE * LOG2E)
        xn_ref[...] = xn.astype(BF16)

    acc = jnp.dot(xn_ref[...], w_ref[...], preferred_element_type=F32)
    cos, sg = cos_ref[...], sg_ref[...]
    for h in range(heads):
        lo = h * QK_PAD
        o_ref[:, lo:lo + QK_NOPE] = acc[:, lo:lo + QK_NOPE].astype(o_ref.dtype)
        o_ref[:, lo + QK_NOPE:lo + QK_PAD] = _rope(
            acc[:, lo + QK_NOPE:lo + QK_PAD], cos, sg).astype(o_ref.dtype)


def _qproj(u, g, w, l, cos4, sg4, *, seq, lat0, q_lora, bm=1024, heads=8):
    t = u.shape[0]
    n = w.shape[-1]
    bm = _tile(seq, bm, BF16_ROWS)
    heads = min(heads, n // QK_PAD)
    bn = heads * QK_PAD
    spb = seq // bm
    assert lat0 % q_lora == 0
    cq_blk = lat0 // q_lora
    tab = pl.BlockSpec((bm, LANES), lambda i, j: (i % spb, 0))
    gspec, g3 = _layer_vec(g, l)
    return pl.pallas_call(
        functools.partial(_qproj_kernel, heads=heads),
        out_shape=jax.ShapeDtypeStruct((t, n), BF16),
        grid=(t // bm, n // bn),
        in_specs=[pl.BlockSpec((bm, q_lora), lambda i, j: (i, cq_blk)),
                  gspec,
                  pl.BlockSpec((None, q_lora, bn), lambda i, j: (l, 0, j)),
                  tab, tab],
        out_specs=pl.BlockSpec((bm, bn), lambda i, j: (i, j)),
        scratch_shapes=[pltpu.VMEM((bm, q_lora), BF16)],
        compiler_params=_params(("parallel", "arbitrary")),
        name="qproj",
    )(u, g3, w, cos4, sg4)


def _kvproj_kernel(c_ref, kr_ref, g_ref, w_ref, cos_ref, sg_ref, k_ref, v_ref, krr_ref,
                   xn_ref, *, heads):
    @pl.when(pl.program_id(1) == 0)
    def _():
        xn_ref[...] = _rms(c_ref[...].astype(F32), g_ref[...]).astype(BF16)
        krr_ref[...] = _rope(kr_ref[...].astype(F32), cos_ref[...], sg_ref[...]).astype(BF16)

    kv = jnp.dot(xn_ref[...], w_ref[...], preferred_element_type=F32)
    width = QK_NOPE + V_HEAD
    for h in range(heads):
        k_ref[:, h * QK_NOPE:(h + 1) * QK_NOPE] = kv[:, h * width:h * width + QK_NOPE].astype(
            k_ref.dtype)
        v_ref[:, h * V_HEAD:(h + 1) * V_HEAD] = kv[:, h * width + QK_NOPE:(h + 1) * width].astype(
            v_ref.dtype)


def _kvproj(u, g, w, l, cos4, sg4, *, seq, lat0, q_lora, kv_lora, bm=1024, heads=8):
    t = u.shape[0]
    width = QK_NOPE + V_HEAD
    n_heads = w.shape[-1] // width
    bm = _tile(seq, bm, BF16_ROWS)
    heads = min(heads, n_heads)
    spb = seq // bm
    ckv0 = lat0 + q_lora
    assert ckv0 % kv_lora == 0 and (ckv0 + kv_lora) % LANES == 0
    ckv_blk = ckv0 // kv_lora
    kr_blk = (ckv0 + kv_lora) // LANES
    tab = pl.BlockSpec((bm, LANES), lambda i, j: (i % spb, 0))
    gspec, g3 = _layer_vec(g, l)
    return pl.pallas_call(
        functools.partial(_kvproj_kernel, heads=heads),
        out_shape=(jax.ShapeDtypeStruct((t, n_heads * QK_NOPE), BF16),
                   jax.ShapeDtypeStruct((t, n_heads * V_HEAD), BF16),
                   jax.ShapeDtypeStruct((t, LANES), BF16)),
        grid=(t // bm, n_heads // heads),
        in_specs=[pl.BlockSpec((bm, kv_lora), lambda i, j: (i, ckv_blk)),
                  pl.BlockSpec((bm, LANES), lambda i, j: (i, kr_blk)),
                  gspec,
                  pl.BlockSpec((None, kv_lora, heads * width), lambda i, j: (l, 0, j)),
                  tab, tab],
        out_specs=(pl.BlockSpec((bm, heads * QK_NOPE), lambda i, j: (i, j)),
                   pl.BlockSpec((bm, heads * V_HEAD), lambda i, j: (i, j)),
                   pl.BlockSpec((bm, LANES), lambda i, j: (i, 0))),
        scratch_shapes=[pltpu.VMEM((bm, kv_lora), BF16)],
        compiler_params=_params(("parallel", "arbitrary")),
        name="kvproj",
    )(u, u, g3, w, cos4, sg4)


def _attn_kernel(q_ref, kn_ref, kr_ref, v_ref, o_ref, k_scr, vt_scr, qt_scr, sa_scr, sb_scr,
                 ma_scr, mb_scr, *, tk, nk, group):
    @pl.when(pl.program_id(2) == 0)
    def _():
        def build(j, c):
            rows = pl.ds(pl.multiple_of(j * tk, tk), tk)
            k_scr[rows, :QK_NOPE] = kn_ref[rows, :]
            k_scr[rows, QK_NOPE:] = kr_ref[rows, :]
            vt_scr[:, rows] = v_ref[rows, :].T
            return c
        lax.fori_loop(0, nk, build, 0)

    qt_scr[...] = q_ref[...].T
    tq = qt_scr.shape[1]

    def chunk(j):
        return pl.ds(j * tk if isinstance(j, int) else pl.multiple_of(j * tk, tk), tk)

    early_max = nk > group

    def scores(j, buf):
        s_ref, mx_ref = buf
        s = jnp.dot(k_scr[chunk(j), :], qt_scr[...], preferred_element_type=F32)
        s_ref[...] = s
        if early_max:
            mx_ref[...] = jnp.max(s, axis=0, keepdims=True)

    def update(j, buf, carry):
        s_ref, mx_ref = buf
        m, l, acc = carry
        mx = mx_ref[...] if early_max else jnp.max(s_ref[...], axis=0, keepdims=True)
        m_new = jnp.maximum(m, mx)
        alpha = jnp.exp2(m - m_new)
        p = jnp.exp2(s_ref[...] - m_new)
        l = alpha * l + jnp.sum(p, axis=0, keepdims=True)
        acc = alpha * acc + jnp.dot(vt_scr[:, chunk(j)], p.astype(BF16),
                                    preferred_element_type=F32)
        return m_new, l, acc

    bufs = ((sa_scr, ma_scr), (sb_scr, mb_scr))

    def trip(i, carry):
        j0 = group * i
        for g in range(group):
            scores(j0 + g + 1, bufs[(g + 1) % 2])
            carry = update(j0 + g, bufs[g % 2], carry)
        return carry

    carry = (jnp.full((1, tq), -jnp.inf, F32), jnp.zeros((1, tq), F32),
             jnp.zeros((V_HEAD, tq), F32))
    scores(0, bufs[0])
    n_trips = (nk - 1) // group
    carry = lax.fori_loop(0, n_trips, trip, carry)
    for j in range(group * n_trips, nk):
        if j + 1 < nk:
            scores(j + 1, bufs[(j + 1) % 2])
        carry = update(j, bufs[j % 2], carry)
    _, l, acc = carry
    o_ref[...] = (acc / l).T.astype(o_ref.dtype)


def _attention(q, kn, krr, v, *, seq, tq=1024, tq_whole=2048, tk=512, group=4):
    t = q.shape[0]
    n_heads = q.shape[1] // QK_PAD
    tq = seq if seq <= tq_whole else _tile(seq, tq, LANES)
    tk = _tile(seq, tk, LANES)
    assert group % 2 == 0
    qpb = seq // tq
    return pl.pallas_call(
        functools.partial(_attn_kernel, tk=tk, nk=seq // tk, group=group),
        out_shape=jax.ShapeDtypeStruct((t, n_heads * V_HEAD), BF16),
        grid=(t // seq, n_heads, qpb),
        in_specs=[pl.BlockSpec((tq, QK_PAD), lambda b, h, i: (b * qpb + i, h)),
                  pl.BlockSpec((seq, QK_NOPE), lambda b, h, i: (b, h)),
                  pl.BlockSpec((seq, LANES), lambda b, h, i: (b, 0)),
                  pl.BlockSpec((seq, V_HEAD), lambda b, h, i: (b, h))],
        out_specs=pl.BlockSpec((tq, V_HEAD), lambda b, h, i: (b * qpb + i, h)),
        scratch_shapes=[pltpu.VMEM((seq, QK_PAD), BF16), pltpu.VMEM((V_HEAD, seq), BF16),
                        pltpu.VMEM((QK_PAD, tq), BF16),
                        pltpu.VMEM((tk, tq), F32), pltpu.VMEM((tk, tq), F32),
                        pltpu.VMEM((1, tq), F32), pltpu.VMEM((1, tq), F32)],
        compiler_params=_params(("parallel", "parallel", "arbitrary")),
        name="attention",
    )(q, kn, krr, v)


def _conv_kernel(a_ref, g_ref, ap_ref, gp_ref, an_ref, gn_ref, w_ref, b_ref, lg_ref, lb_ref,
                 o_ref, h_ref, c_ref, *, ts, tiles_per_seq, width, rows):
    i = pl.program_id(0)
    halo = BF16_ROWS

    def glu(a, g):
        return a[...].astype(F32) * jax.nn.sigmoid(g[...].astype(F32))

    h_ref[halo:halo + ts, :] = glu(a_ref, g_ref)
    first = (i % tiles_per_seq) == 0
    last = (i % tiles_per_seq) == tiles_per_seq - 1
    h_ref[0:halo, :] = jnp.where(first, 0.0, glu(ap_ref, gp_ref))
    h_ref[halo + ts:2 * halo + ts, :] = jnp.where(last, 0.0, glu(an_ref, gn_ref))

    pad = (width - 1) // 2
    base = halo - pad
    span = ((base + width - 1) // 8) * 8
    wrows = rows + span + 8
    assert wrows <= rows + 2 * halo
    n_chunks = a_ref.shape[1] // LANES

    def lane_chunk(c, carry):
        lanes = pl.ds(pl.multiple_of(c * LANES, LANES), LANES)
        wts = [w_ref[pl.ds(k, 1), lanes] for k in range(width)]
        bias = b_ref[:, lanes]
        for r0 in range(0, ts, rows):
            aligned = h_ref[pl.ds(r0, wrows), lanes]
            acc = jnp.zeros((rows, LANES), F32)
            for r in range(8):
                taps = [k for k in range(width) if (base + k) % 8 == r]
                if not taps:
                    continue
                win = aligned if r == 0 else pltpu.roll(aligned, wrows - r, 0)
                for k in taps:
                    off = (base + k) - r
                    acc = acc + wts[k] * win[off:off + rows, :]
            c_ref[pl.ds(r0, rows), lanes] = acc + bias
        return carry

    lax.fori_loop(0, n_chunks, lane_chunk, 0)

    y = c_ref[...]
    mu = jnp.mean(y, axis=-1, keepdims=True)
    yc = y - mu
    var = jnp.mean(yc * yc, axis=-1, keepdims=True)
    z = yc * lax.rsqrt(var + EPS) * lg_ref[...] + lb_ref[...]
    o_ref[...] = (z * jax.nn.sigmoid(z)).astype(o_ref.dtype)


def _conv_branch(u_main, conv_w, conv_b, ln_g, ln_b, l, *, seq, d_conv, ts=256, rows=64):
    t = u_main.shape[0]
    width = conv_w.shape[1]
    halo = BF16_ROWS
    assert (width - 1) // 2 <= halo
    ts = _tile(seq, ts, halo)
    rows = _tile(ts, rows, 8)
    hb = ts // halo
    last_blk = t // halo - 1
    cur = lambda c: pl.BlockSpec((ts, d_conv), lambda i: (i, c))
    prev = lambda c: pl.BlockSpec((halo, d_conv), lambda i: (jnp.maximum(i * hb - 1, 0), c))
    nxt = lambda c: pl.BlockSpec((halo, d_conv), lambda i: (jnp.minimum((i + 1) * hb, last_blk), c))
    bspec, b3 = _layer_vec(conv_b, l)
    gspec, g3 = _layer_vec(ln_g, l)
    lspec, l3 = _layer_vec(ln_b, l)
    return pl.pallas_call(
        functools.partial(_conv_kernel, ts=ts, tiles_per_seq=seq // ts, width=width, rows=rows),
        out_shape=jax.ShapeDtypeStruct((t, d_conv), BF16),
        grid=(t // ts,),
        in_specs=[cur(0), cur(1), prev(0), prev(1), nxt(0), nxt(1),
                  pl.BlockSpec((None, width, d_conv), lambda i: (l, 0, 0)), bspec, gspec, lspec],
        out_specs=pl.BlockSpec((ts, d_conv), lambda i: (i, 0)),
        scratch_shapes=[pltpu.VMEM((ts + 2 * halo, d_conv), F32),
                        pltpu.VMEM((ts, d_conv), F32)],
        compiler_params=_params(("parallel",)),
        name="conv_branch",
    )(u_main, u_main, u_main, u_main, u_main, u_main, conv_w, b3, g3, l3)


def _merge_kernel(hc_ref, o_ref, wc_ref, wo_ref, gc_ref, ga_ref, out_ref):
    yc = jnp.dot(hc_ref[...], wc_ref[...], preferred_element_type=F32)
    ya = jnp.dot(o_ref[...], wo_ref[...], preferred_element_type=F32)
    gc = jax.nn.sigmoid(gc_ref[...].astype(F32))
    ga = jax.nn.sigmoid(ga_ref[...].astype(F32))
    out_ref[...] = (gc * yc + ga * ya).astype(out_ref.dtype)


def _merge(hc, o, wc, wo, l, u_main, *, d_conv, bm=1024, bn=512):
    t = hc.shape[0]
    d = wc.shape[-1]
    bm, bn = _tile(t, bm, BF16_ROWS), _tile(d, bn, LANES)
    assert (2 * d_conv) % bn == 0
    g0 = 2 * d_conv // bn
    nb = d // bn
    return pl.pallas_call(
        _merge_kernel,
        out_shape=jax.ShapeDtypeStruct((t, d), BF16),
        grid=(t // bm, nb),
        in_specs=[pl.BlockSpec((bm, hc.shape[1]), lambda i, j: (i, 0)),
                  pl.BlockSpec((bm, o.shape[1]), lambda i, j: (i, 0)),
                  pl.BlockSpec((None, wc.shape[1], bn), lambda i, j: (l, 0, j)),
                  pl.BlockSpec((None, wo.shape[1], bn), lambda i, j: (l, 0, j)),
                  pl.BlockSpec((bm, bn), lambda i, j: (i, g0 + j)),
                  pl.BlockSpec((bm, bn), lambda i, j: (i, g0 + nb + j))],
        out_specs=pl.BlockSpec((bm, bn), lambda i, j: (i, j)),
        compiler_params=_params(("parallel", "parallel")),
        name="merge",
    )(hc, o, wc, wo, u_main, u_main)


def _gateup_kernel(h_ref, wg_ref, wu_ref, o_ref):
    h = h_ref[...]
    gate = jnp.dot(h, wg_ref[...], preferred_element_type=F32)
    up = jnp.dot(h, wu_ref[...], preferred_element_type=F32)
    o_ref[...] = (gate * jax.nn.sigmoid(gate) * up).astype(o_ref.dtype)


def _gateup(h, wg, wu, l, *, bm=1024, bn=512):
    t, d = h.shape
    f = wg.shape[-1]
    bm, bn = _tile(t, bm, BF16_ROWS), min(bn, f)
    wspec = pl.BlockSpec((None, d, bn), lambda i, j: (l, 0, j))
    return pl.pallas_call(
        _gateup_kernel,
        out_shape=jax.ShapeDtypeStruct((t, f), BF16),
        grid=(t // bm, pl.cdiv(f, bn)),
        in_specs=[pl.BlockSpec((bm, d), lambda i, j: (i, 0)), wspec, wspec],
        out_specs=pl.BlockSpec((bm, bn), lambda i, j: (i, j)),
        compiler_params=_params(("parallel", "parallel")),
        name="gateup",
    )(h, wg, wu)


def _q_column_selector(n_heads):
    src = np.full((n_heads, QK_PAD), -1, np.int32)
    base = np.arange(n_heads)[:, None] * QK_HEAD
    src[:, :QK_NOPE + HALF_ROPE] = base + np.arange(QK_NOPE + HALF_ROPE)
    lo = QK_NOPE + 2 * HALF_ROPE
    src[:, lo:lo + HALF_ROPE] = base + QK_NOPE + HALF_ROPE + np.arange(HALF_ROPE)
    rows = lax.broadcasted_iota(jnp.int32, (n_heads * QK_HEAD, n_heads * QK_PAD), 0)
    return (rows == jnp.asarray(src.reshape(1, -1))).astype(BF16)


def _prep_weights(w_in, w_conv_out, w_uq, w_ukv, w_o_attn, w_out, w_gate, w_up, w_down,
                  *, d_conv, q_lora, kv_lora):
    depth, d, _ = w_in.shape
    off_cq = 2 * d_conv
    off_ckv = off_cq + q_lora
    off_kr = off_ckv + kv_lora
    off_gate = off_kr + QK_ROPE
    zr = jnp.zeros((depth, d, HALF_ROPE), w_in.dtype)
    w_main = jnp.concatenate([w_in[..., :off_cq], w_in[..., off_gate:]], axis=-1).astype(BF16)
    w_lat = jnp.concatenate([w_in[..., off_cq:off_kr],
                             w_in[..., off_kr:off_kr + HALF_ROPE], zr,
                             w_in[..., off_kr + HALF_ROPE:off_gate], zr], axis=-1).astype(BF16)

    n_heads = w_uq.shape[-1] // QK_HEAD
    wq = _matmul(w_uq.astype(BF16).reshape(depth * q_lora, n_heads * QK_HEAD),
                 _q_column_selector(n_heads), BF16, bm=512, bn=512,
                 name="wq_layout").reshape(depth, q_lora, n_heads * QK_PAD)

    return dict(w_main=w_main, w_lat=w_lat, wq=wq, wkv=w_ukv.astype(BF16),
                wc=w_conv_out.astype(BF16), wo=w_o_attn.astype(BF16), w_out=w_out.astype(BF16),
                wg=w_gate.astype(BF16), wu=w_up.astype(BF16), w_dn=w_down.astype(BF16))


def _rope_tables(seq):
    inv_freq = ROPE_THETA ** (-jnp.arange(0, QK_ROPE, 2, dtype=F32) / QK_ROPE)
    ang = jnp.arange(seq, dtype=F32)[:, None] * inv_freq[None, :]
    cos, sin = jnp.cos(ang), jnp.sin(ang)
    return (jnp.concatenate([cos, cos, cos, cos], axis=-1),
            jnp.concatenate([-sin, -sin, sin, sin], axis=-1))


def _trunk(x3, pw, small, dims):
    b, seq, d = x3.shape
    x = x3.reshape(b * seq, d)
    depth = pw["w_main"].shape[0]
    d_conv, q_lora, kv_lora = dims
    cos4, sg4 = _rope_tables(seq)
    h = _rmsnorm(x, small["pre_norm_mix"], 0)
    for l in range(depth):
        u_main = _matmul(h, pw["w_main"], BF16, bm=1024, bn=1024, name="w_in_main", layer=l)
        u_lat = _matmul(h, pw["w_lat"], BF16, bm=512, bn=pw["w_lat"].shape[-1],
                        name="w_in_latent", layer=l)
        hc = _conv_branch(u_main, small["conv_w"], small["conv_b"], small["conv_ln_g"],
                          small["conv_ln_b"], l, seq=seq, d_conv=d_conv)
        q = _qproj(u_lat, small["q_norm"], pw["wq"], l, cos4, sg4, seq=seq, lat0=0,
                   q_lora=q_lora)
        kn, v, krr = _kvproj(u_lat, small["kv_norm"], pw["wkv"], l, cos4, sg4,
                             seq=seq, lat0=0, q_lora=q_lora, kv_lora=kv_lora)
        o = _attention(q, kn, krr, v, seq=seq)
        mix = _merge(hc, o, pw["wc"], pw["wo"], l, u_main, d_conv=d_conv)
        mixed = _matmul(mix, pw["w_out"], BF16, bm=1024, bn=1024, name="w_out", layer=l)
        x, h = _resnorm(x, mixed, small["post_norm_mix"], l, small["pre_norm_ffn"], l)
        act = _gateup(h, pw["wg"], pw["wu"], l)
        f = _matmul(act, pw["w_dn"], BF16, bm=512, bn=512, name="w_down", layer=l)
        if l + 1 < depth:
            x, h = _resnorm(x, f, small["post_norm_ffn"], l, small["pre_norm_mix"], l + 1)
        else:
            x, _ = _resnorm(x, f, small["post_norm_ffn"], l)
    return x.reshape(b, seq, d)


def kernel(x_prompt, x_sample, pre_norm_mix, w_in, conv_w, conv_b, conv_ln_g, conv_ln_b, w_conv_out, q_norm, w_uq, kv_norm, w_ukv, w_o_attn, w_out, post_norm_mix, pre_norm_ffn, w_gate, w_up, w_down, post_norm_ffn):
    d_conv = conv_w.shape[-1]
    q_lora = q_norm.shape[-1]
    kv_lora = kv_norm.shape[-1]
    pw = _prep_weights(w_in, w_conv_out, w_uq, w_ukv, w_o_attn, w_out, w_gate, w_up, w_down,
                       d_conv=d_conv, q_lora=q_lora, kv_lora=kv_lora)
    small = dict(pre_norm_mix=pre_norm_mix, conv_w=conv_w, conv_b=conv_b, conv_ln_g=conv_ln_g,
                 conv_ln_b=conv_ln_b, q_norm=q_norm, kv_norm=kv_norm, post_norm_mix=post_norm_mix,
                 pre_norm_ffn=pre_norm_ffn, post_norm_ffn=post_norm_ffn)
    dims = (d_conv, q_lora, kv_lora)
    return (_trunk(x_prompt, pw, small, dims), _trunk(x_sample, pw, small, dims))
```

```python
import functools

import numpy as np
import jax
import jax.numpy as jnp
from jax import lax
from jax.experimental import pallas as pl
from jax.experimental.pallas import tpu as pltpu

F32 = jnp.float32
BF16 = jnp.bfloat16

QK_NOPE = 128
QK_ROPE = 64
QK_HEAD = QK_NOPE + QK_ROPE
V_HEAD = 128
ROPE_THETA = 10000.0
ATTN_SCALE = QK_HEAD ** -0.5
LOG2E = 1.4426950408889634
EPS = 1e-6

LANES = 128
HALF_ROPE = QK_ROPE // 2
QK_PAD = QK_NOPE + LANES
BF16_ROWS = 16
VMEM_LIMIT = 56 * 1024 * 1024


def _tile(dim, pref, align):
    if dim <= pref:
        return dim
    t = (pref // align) * align
    while t >= align:
        if dim % t == 0:
            return t
        t -= align
    return dim


def _params(sem):
    return pltpu.CompilerParams(dimension_semantics=sem, vmem_limit_bytes=VMEM_LIMIT)


def _layer_vec(p, l):
    depth, n = p.shape
    return pl.BlockSpec((None, 1, n), lambda *_: (l, 0, 0)), p.reshape(depth, 1, n)


def _rms(x, g):
    return x * lax.rsqrt(jnp.mean(x * x, axis=-1, keepdims=True) + EPS) * g


def _rope(x, cos, sg):
    return x * cos + pltpu.roll(x, 2 * HALF_ROPE, 1) * sg


def _mm_kernel(x_ref, w_ref, o_ref):
    o_ref[...] = jnp.dot(x_ref[...], w_ref[...], preferred_element_type=F32).astype(o_ref.dtype)


def _matmul(x, w, out_dtype, *, bm, bn, name, layer=None):
    m, kdim = x.shape
    n = w.shape[-1]
    bm, bn = _tile(m, bm, 8), min(bn, n)
    if layer is None:
        wspec = pl.BlockSpec((kdim, bn), lambda i, j: (0, j))
    else:
        wspec = pl.BlockSpec((None, kdim, bn), lambda i, j: (layer, 0, j))
    return pl.pallas_call(
        _mm_kernel,
        out_shape=jax.ShapeDtypeStruct((m, n), out_dtype),
        grid=(m // bm, pl.cdiv(n, bn)),
        in_specs=[pl.BlockSpec((bm, kdim), lambda i, j: (i, 0)), wspec],
        out_specs=pl.BlockSpec((bm, bn), lambda i, j: (i, j)),
        compiler_params=_params(("parallel", "parallel")),
        name=name,
    )(x, w)


def _rmsnorm_kernel(x_ref, g_ref, h_ref):
    h_ref[...] = _rms(x_ref[...], g_ref[...]).astype(h_ref.dtype)


def _rmsnorm(x, g, l, *, br=256):
    t, d = x.shape
    br = _tile(t, br, 8)
    gspec, g3 = _layer_vec(g, l)
    return pl.pallas_call(
        _rmsnorm_kernel,
        out_shape=jax.ShapeDtypeStruct((t, d), BF16),
        grid=(t // br,),
        in_specs=[pl.BlockSpec((br, d), lambda i: (i, 0)), gspec],
        out_specs=pl.BlockSpec((br, d), lambda i: (i, 0)),
        compiler_params=_params(("parallel",)),
        name="rmsnorm",
    )(x, g3)


def _resnorm_kernel(x_ref, d_ref, gpost_ref, gnext_ref, xo_ref, h_ref):
    xn = x_ref[...] + _rms(d_ref[...].astype(F32), gpost_ref[...])
    xo_ref[...] = xn
    h_ref[...] = _rms(xn, gnext_ref[...]).astype(h_ref.dtype)


def _resadd_kernel(x_ref, d_ref, gpost_ref, xo_ref):
    xo_ref[...] = x_ref[...] + _rms(d_ref[...].astype(F32), gpost_ref[...])


def _resnorm(x, delta, g_post, l, g_next=None, l_next=None, *, br=256):
    t, d = x.shape
    br = _tile(t, br, 8)
    row = pl.BlockSpec((br, d), lambda i: (i, 0))
    pspec, gp3 = _layer_vec(g_post, l)
    if g_next is None:
        return pl.pallas_call(
            _resadd_kernel,
            out_shape=jax.ShapeDtypeStruct((t, d), F32),
            grid=(t // br,),
            in_specs=[row, row, pspec],
            out_specs=row,
            compiler_params=_params(("parallel",)),
            name="resadd",
        )(x, delta, gp3), None
    nspec, gn3 = _layer_vec(g_next, l_next)
    return pl.pallas_call(
        _resnorm_kernel,
        out_shape=(jax.ShapeDtypeStruct((t, d), F32), jax.ShapeDtypeStruct((t, d), BF16)),
        grid=(t // br,),
        in_specs=[row, row, pspec, nspec],
        out_specs=(row, row),
        compiler_params=_params(("parallel",)),
        name="resnorm",
    )(x, delta, gp3, gn3)


def _qproj_kernel(c_ref, g_ref, w_ref, cos_ref, sg_ref, o_ref, xn_ref, *, heads):
    @pl.when(pl.program_id(1) == 0)
    def _():
        xn = _rms(c_ref[...].astype(F32), g_ref[...]) * (ATTN_SCALE * LOG2E)
        xn_ref[...] = xn.astype(BF16)

    acc = jnp.dot(xn_ref[...], w_ref[...], preferred_element_type=F32)
    cos, sg = cos_ref[...], sg_ref[...]
    for h in range(heads):
        lo = h * QK_PAD
        o_ref[:, lo:lo + QK_NOPE] = acc[:, lo:lo + QK_NOPE].astype(o_ref.dtype)
        o_ref[:, lo + QK_NOPE:lo + QK_PAD] = _rope(
            acc[:, lo + QK_NOPE:lo + QK_PAD], cos, sg).astype(o_ref.dtype)


def _qproj(u, g, w, l, cos4, sg4, *, seq, lat0, q_lora, bm=1024, heads=8):
    t = u.shape[0]
    n = w.shape[-1]
    bm = _tile(seq, bm, BF16_ROWS)
    heads = min(heads, n // QK_PAD)
    bn = heads * QK_PAD
    spb = seq // bm
    assert lat0 % q_lora == 0
    cq_blk = lat0 // q_lora
    tab = pl.BlockSpec((bm, LANES), lambda i, j: (i % spb, 0))
    gspec, g3 = _layer_vec(g, l)
    return pl.pallas_call(
        functools.partial(_qproj_kernel, heads=heads),
        out_shape=jax.ShapeDtypeStruct((t, n), BF16),
        grid=(t // bm, n // bn),
        in_specs=[pl.BlockSpec((bm, q_lora), lambda i, j: (i, cq_blk)),
                  gspec,
                  pl.BlockSpec((None, q_lora, bn), lambda i, j: (l, 0, j)),
                  tab, tab],
        out_specs=pl.BlockSpec((bm, bn), lambda i, j: (i, j)),
        scratch_shapes=[pltpu.VMEM((bm, q_lora), BF16)],
        compiler_params=_params(("parallel", "arbitrary")),
        name="qproj",
    )(u, g3, w, cos4, sg4)


def _kvproj_kernel(c_ref, kr_ref, g_ref, w_ref, cos_ref, sg_ref, k_ref, v_ref, krr_ref,
                   xn_ref, *, heads):
    @pl.when(pl.program_id(1) == 0)
    def _():
        xn_ref[...] = _rms(c_ref[...].astype(F32), g_ref[...]).astype(BF16)
        krr_ref[...] = _rope(kr_ref[...].astype(F32), cos_ref[...], sg_ref[...]).astype(BF16)

    kv = jnp.dot(xn_ref[...], w_ref[...], preferred_element_type=F32)
    width = QK_NOPE + V_HEAD
    for h in range(heads):
        k_ref[:, h * QK_NOPE:(h + 1) * QK_NOPE] = kv[:, h * width:h * width + QK_NOPE].astype(
            k_ref.dtype)
        v_ref[:, h * V_HEAD:(h + 1) * V_HEAD] = kv[:, h * width + QK_NOPE:(h + 1) * width].astype(
            v_ref.dtype)


def _kvproj(u, g, w, l, cos4, sg4, *, seq, lat0, q_lora, kv_lora, bm=1024, heads=8):
    t = u.shape[0]
    width = QK_NOPE + V_HEAD
    n_heads = w.shape[-1] // width
    bm = _tile(seq, bm, BF16_ROWS)
    heads = min(heads, n_heads)
    spb = seq // bm
    ckv0 = lat0 + q_lora
    assert ckv0 % kv_lora == 0 and (ckv0 + kv_lora) % LANES == 0
    ckv_blk = ckv0 // kv_lora
    kr_blk = (ckv0 + kv_lora) // LANES
    tab = pl.BlockSpec((bm, LANES), lambda i, j: (i % spb, 0))
    gspec, g3 = _layer_vec(g, l)
    return pl.pallas_call(
        functools.partial(_kvproj_kernel, heads=heads),
        out_shape=(jax.ShapeDtypeStruct((t, n_heads * QK_NOPE), BF16),
                   jax.ShapeDtypeStruct((t, n_heads * V_HEAD), BF16),
                   jax.ShapeDtypeStruct((t, LANES), BF16)),
        grid=(t // bm, n_heads // heads),
        in_specs=[pl.BlockSpec((bm, kv_lora), lambda i, j: (i, ckv_blk)),
                  pl.BlockSpec((bm, LANES), lambda i, j: (i, kr_blk)),
                  gspec,
                  pl.BlockSpec((None, kv_lora, heads * width), lambda i, j: (l, 0, j)),
                  tab, tab],
        out_specs=(pl.BlockSpec((bm, heads * QK_NOPE), lambda i, j: (i, j)),
                   pl.BlockSpec((bm, heads * V_HEAD), lambda i, j: (i, j)),
                   pl.BlockSpec((bm, LANES), lambda i, j: (i, 0))),
        scratch_shapes=[pltpu.VMEM((bm, kv_lora), BF16)],
        compiler_params=_params(("parallel", "arbitrary")),
        name="kvproj",
    )(u, u, g3, w, cos4, sg4)


def _attn_kernel(q_ref, kn_ref, kr_ref, v_ref, o_ref, k_scr, vt_scr, qt_scr, sa_scr, sb_scr,
                 ma_scr, mb_scr, acc_scr, *, tk, nk, group):
    @pl.when(pl.program_id(2) == 0)
    def _():
        def build(j, c):
            rows = pl.ds(pl.multiple_of(j * tk, tk), tk)
            k_scr[rows, :QK_NOPE] = kn_ref[rows, :]
            k_scr[rows, QK_NOPE:] = kr_ref[rows, :]
            vt_scr[:, rows] = v_ref[rows, :].T
            return c
        lax.fori_loop(0, nk, build, 0)

    qt_scr[...] = q_ref[...].T
    tq = qt_scr.shape[1]

    def chunk(j):
        return pl.ds(j * tk if isinstance(j, int) else pl.multiple_of(j * tk, tk), tk)

    early_max = nk > group

    def scores(j, buf):
        s_ref, mx_ref = buf
        s = jnp.dot(k_scr[chunk(j), :], qt_scr[...], preferred_element_type=F32)
        s_ref[...] = s
        if early_max:
            mx_ref[...] = jnp.max(s, axis=0, keepdims=True)

    def update(j, buf, carry):
        s_ref, mx_ref = buf
        m, l = carry
        mx = mx_ref[...] if early_max else jnp.max(s_ref[...], axis=0, keepdims=True)
        m_new = jnp.maximum(m, mx)
        alpha = jnp.exp2(m - m_new)
        p = jnp.exp2(s_ref[...] - m_new)
        l = alpha * l + jnp.sum(p, axis=0, keepdims=True)
        acc_scr[...] = alpha * acc_scr[...] + jnp.dot(vt_scr[:, chunk(j)], p.astype(BF16),
                                                      preferred_element_type=F32)
        return m_new, l

    bufs = ((sa_scr, ma_scr), (sb_scr, mb_scr))

    def trip(i, carry):
        j0 = group * i
        for g in range(group):
            scores(j0 + g + 1, bufs[(g + 1) % 2])
            carry = update(j0 + g, bufs[g % 2], carry)
        return carry

    carry = (jnp.full((1, tq), -jnp.inf, F32), jnp.zeros((1, tq), F32))
    acc_scr[...] = jnp.zeros_like(acc_scr)
    scores(0, bufs[0])
    n_trips = (nk - 1) // group
    carry = lax.fori_loop(0, n_trips, trip, carry)
    for j in range(group * n_trips, nk):
        if j + 1 < nk:
            scores(j + 1, bufs[(j + 1) % 2])
        carry = update(j, bufs[j % 2], carry)
    _, l = carry
    o_ref[...] = (acc_scr[...] / l).T.astype(o_ref.dtype)


def _attention(q, kn, krr, v, *, seq, tq=1024, tq_whole=2048, tk=512, group=4):
    t = q.shape[0]
    n_heads = q.shape[1] // QK_PAD
    tq = seq if seq <= tq_whole else _tile(seq, tq, LANES)
    tk = _tile(seq, tk, LANES)
    assert group % 2 == 0
    qpb = seq // tq
    return pl.pallas_call(
        functools.partial(_attn_kernel, tk=tk, nk=seq // tk, group=group),
        out_shape=jax.ShapeDtypeStruct((t, n_heads * V_HEAD), BF16),
        grid=(t // seq, n_heads, qpb),
        in_specs=[pl.BlockSpec((tq, QK_PAD), lambda b, h, i: (b * qpb + i, h)),
                  pl.BlockSpec((seq, QK_NOPE), lambda b, h, i: (b, h)),
                  pl.BlockSpec((seq, LANES), lambda b, h, i: (b, 0)),
                  pl.BlockSpec((seq, V_HEAD), lambda b, h, i: (b, h))],
        out_specs=pl.BlockSpec((tq, V_HEAD), lambda b, h, i: (b * qpb + i, h)),
        scratch_shapes=[pltpu.VMEM((seq, QK_PAD), BF16), pltpu.VMEM((V_HEAD, seq), BF16),
                        pltpu.VMEM((QK_PAD, tq), BF16),
                        pltpu.VMEM((tk, tq), F32), pltpu.VMEM((tk, tq), F32),
                        pltpu.VMEM((1, tq), F32), pltpu.VMEM((1, tq), F32),
                        pltpu.VMEM((V_HEAD, tq), F32)],
        compiler_params=_params(("parallel", "parallel", "arbitrary")),
        name="attention",
    )(q, kn, krr, v)


def _conv_kernel(a_ref, g_ref, ap_ref, gp_ref, an_ref, gn_ref, w_ref, b_ref, lg_ref, lb_ref,
                 o_ref, h_ref, c_ref, *, ts, tiles_per_seq, width, rows):
    i = pl.program_id(0)
    halo = BF16_ROWS

    def glu(a, g):
        return a[...].astype(F32) * jax.nn.sigmoid(g[...].astype(F32))

    h_ref[halo:halo + ts, :] = glu(a_ref, g_ref)
    first = (i % tiles_per_seq) == 0
    last = (i % tiles_per_seq) == tiles_per_seq - 1
    h_ref[0:halo, :] = jnp.where(first, 0.0, glu(ap_ref, gp_ref))
    h_ref[halo + ts:2 * halo + ts, :] = jnp.where(last, 0.0, glu(an_ref, gn_ref))

    pad = (width - 1) // 2
    base = halo - pad
    span = ((base + width - 1) // 8) * 8
    wrows = rows + span + 8
    assert wrows <= rows + 2 * halo
    n_chunks = a_ref.shape[1] // LANES

    def lane_chunk(c, carry):
        lanes = pl.ds(pl.multiple_of(c * LANES, LANES), LANES)
        wts = [w_ref[pl.ds(k, 1), lanes] for k in range(width)]
        bias = b_ref[:, lanes]
        for r0 in range(0, ts, rows):
            aligned = h_ref[pl.ds(r0, wrows), lanes]
            acc = jnp.zeros((rows, LANES), F32)
            for r in range(8):
                taps = [k for k in range(width) if (base + k) % 8 == r]
                if not taps:
                    continue
                win = aligned if r == 0 else pltpu.roll(aligned, wrows - r, 0)
                for k in taps:
                    off = (base + k) - r
                    acc = acc + wts[k] * win[off:off + rows, :]
            c_ref[pl.ds(r0, rows), lanes] = acc + bias
        return carry

    lax.fori_loop(0, n_chunks, lane_chunk, 0)

    y = c_ref[...]
    mu = jnp.mean(y, axis=-1, keepdims=True)
    yc = y - mu
    var = jnp.mean(yc * yc, axis=-1, keepdims=True)
    z = yc * lax.rsqrt(var + EPS) * lg_ref[...] + lb_ref[...]
    o_ref[...] = (z * jax.nn.sigmoid(z)).astype(o_ref.dtype)


def _conv_branch(u_main, conv_w, conv_b, ln_g, ln_b, l, *, seq, d_conv, ts=256, rows=64):
    t = u_main.shape[0]
    width = conv_w.shape[1]
    halo = BF16_ROWS
    assert (width - 1) // 2 <= halo
    ts = _tile(seq, ts, halo)
    rows = _tile(ts, rows, 8)
    hb = ts // halo
    last_blk = t // halo - 1
    cur = lambda c: pl.BlockSpec((ts, d_conv), lambda i: (i, c))
    prev = lambda c: pl.BlockSpec((halo, d_conv), lambda i: (jnp.maximum(i * hb - 1, 0), c))
    nxt = lambda c: pl.BlockSpec((halo, d_conv), lambda i: (jnp.minimum((i + 1) * hb, last_blk), c))
    bspec, b3 = _layer_vec(conv_b, l)
    gspec, g3 = _layer_vec(ln_g, l)
    lspec, l3 = _layer_vec(ln_b, l)
    return pl.pallas_call(
        functools.partial(_conv_kernel, ts=ts, tiles_per_seq=seq // ts, width=width, rows=rows),
        out_shape=jax.ShapeDtypeStruct((t, d_conv), BF16),
        grid=(t // ts,),
        in_specs=[cur(0), cur(1), prev(0), prev(1), nxt(0), nxt(1),
                  pl.BlockSpec((None, width, d_conv), lambda i: (l, 0, 0)), bspec, gspec, lspec],
        out_specs=pl.BlockSpec((ts, d_conv), lambda i: (i, 0)),
        scratch_shapes=[pltpu.VMEM((ts + 2 * halo, d_conv), F32),
                        pltpu.VMEM((ts, d_conv), F32)],
        compiler_params=_params(("parallel",)),
        name="conv_branch",
    )(u_main, u_main, u_main, u_main, u_main, u_main, conv_w, b3, g3, l3)


def _merge_kernel(hc_ref, o_ref, wc_ref, wo_ref, gc_ref, ga_ref, out_ref):
    yc = jnp.dot(hc_ref[...], wc_ref[...], preferred_element_type=F32)
    ya = jnp.dot(o_ref[...], wo_ref[...], preferred_element_type=F32)
    gc = jax.nn.sigmoid(gc_ref[...].astype(F32))
    ga = jax.nn.sigmoid(ga_ref[...].astype(F32))
    out_ref[...] = (gc * yc + ga * ya).astype(out_ref.dtype)


def _merge(hc, o, wc, wo, l, u_main, *, d_conv, bm=1024, bn=512):
    t = hc.shape[0]
    d = wc.shape[-1]
    bm, bn = _tile(t, bm, BF16_ROWS), _tile(d, bn, LANES)
    assert (2 * d_conv) % bn == 0
    g0 = 2 * d_conv // bn
    nb = d // bn
    return pl.pallas_call(
        _merge_kernel,
        out_shape=jax.ShapeDtypeStruct((t, d), BF16),
        grid=(t // bm, nb),
        in_specs=[pl.BlockSpec((bm, hc.shape[1]), lambda i, j: (i, 0)),
                  pl.BlockSpec((bm, o.shape[1]), lambda i, j: (i, 0)),
                  pl.BlockSpec((None, wc.shape[1], bn), lambda i, j: (l, 0, j)),
                  pl.BlockSpec((None, wo.shape[1], bn), lambda i, j: (l, 0, j)),
                  pl.BlockSpec((bm, bn), lambda i, j: (i, g0 + j)),
                  pl.BlockSpec((bm, bn), lambda i, j: (i, g0 + nb + j))],
        out_specs=pl.BlockSpec((bm, bn), lambda i, j: (i, j)),
        compiler_params=_params(("parallel", "parallel")),
        name="merge",
    )(hc, o, wc, wo, u_main, u_main)


def _gateup_kernel(h_ref, wg_ref, wu_ref, o_ref):
    h = h_ref[...]
    gate = jnp.dot(h, wg_ref[...], preferred_element_type=F32)
    up = jnp.dot(h, wu_ref[...], preferred_element_type=F32)
    o_ref[...] = (gate * jax.nn.sigmoid(gate) * up).astype(o_ref.dtype)


def _gateup(h, wg, wu, l, *, bm=1024, bn=512):
    t, d = h.shape
    f = wg.shape[-1]
    bm, bn = _tile(t, bm, BF16_ROWS), min(bn, f)
    wspec = pl.BlockSpec((None, d, bn), lambda i, j: (l, 0, j))
    return pl.pallas_call(
        _gateup_kernel,
        out_shape=jax.ShapeDtypeStruct((t, f), BF16),
        grid=(t // bm, pl.cdiv(f, bn)),
        in_specs=[pl.BlockSpec((bm, d), lambda i, j: (i, 0)), wspec, wspec],
        out_specs=pl.BlockSpec((bm, bn), lambda i, j: (i, j)),
        compiler_params=_params(("parallel", "parallel")),
        name="gateup",
    )(h, wg, wu)


def _q_column_selector(n_heads):
    src = np.full((n_heads, QK_PAD), -1, np.int32)
    base = np.arange(n_heads)[:, None] * QK_HEAD
    src[:, :QK_NOPE + HALF_ROPE] = base + np.arange(QK_NOPE + HALF_ROPE)
    lo = QK_NOPE + 2 * HALF_ROPE
    src[:, lo:lo + HALF_ROPE] = base + QK_NOPE + HALF_ROPE + np.arange(HALF_ROPE)
    rows = lax.broadcasted_iota(jnp.int32, (n_heads * QK_HEAD, n_heads * QK_PAD), 0)
    return (rows == jnp.asarray(src.reshape(1, -1))).astype(BF16)


def _prep_weights(w_in, w_conv_out, w_uq, w_ukv, w_o_attn, w_out, w_gate, w_up, w_down,
                  *, d_conv, q_lora, kv_lora):
    depth, d, _ = w_in.shape
    off_cq = 2 * d_conv
    off_ckv = off_cq + q_lora
    off_kr = off_ckv + kv_lora
    off_gate = off_kr + QK_ROPE
    zr = jnp.zeros((depth, d, HALF_ROPE), w_in.dtype)
    w_main = jnp.concatenate([w_in[..., :off_cq], w_in[..., off_gate:]], axis=-1).astype(BF16)
    w_lat = jnp.concatenate([w_in[..., off_cq:off_kr],
                             w_in[..., off_kr:off_kr + HALF_ROPE], zr,
                             w_in[..., off_kr + HALF_ROPE:off_gate], zr], axis=-1).astype(BF16)

    n_heads = w_uq.shape[-1] // QK_HEAD
    wq = _matmul(w_uq.astype(BF16).reshape(depth * q_lora, n_heads * QK_HEAD),
                 _q_column_selector(n_heads), BF16, bm=512, bn=512,
                 name="wq_layout").reshape(depth, q_lora, n_heads * QK_PAD)

    return dict(w_main=w_main, w_lat=w_lat, wq=wq, wkv=w_ukv.astype(BF16),
                wc=w_conv_out.astype(BF16), wo=w_o_attn.astype(BF16), w_out=w_out.astype(BF16),
                wg=w_gate.astype(BF16), wu=w_up.astype(BF16), w_dn=w_down.astype(BF16))


def _rope_tables(seq):
    inv_freq = ROPE_THETA ** (-jnp.arange(0, QK_ROPE, 2, dtype=F32) / QK_ROPE)
    ang = jnp.arange(seq, dtype=F32)[:, None] * inv_freq[None, :]
    cos, sin = jnp.cos(ang), jnp.sin(ang)
    return (jnp.concatenate([cos, cos, cos, cos], axis=-1),
            jnp.concatenate([-sin, -sin, sin, sin], axis=-1))


def _trunk(x3, pw, small, dims):
    b, seq, d = x3.shape
    x = x3.reshape(b * seq, d)
    depth = pw["w_main"].shape[0]
    d_conv, q_lora, kv_lora = dims
    cos4, sg4 = _rope_tables(seq)
    h = _rmsnorm(x, small["pre_norm_mix"], 0)
    for l in range(depth):
        u_main = _matmul(h, pw["w_main"], BF16, bm=1024, bn=1024, name="w_in_main", layer=l)
        u_lat = _matmul(h, pw["w_lat"], BF16, bm=512, bn=pw["w_lat"].shape[-1],
                        name="w_in_latent", layer=l)
        hc = _conv_branch(u_main, small["conv_w"], small["conv_b"], small["conv_ln_g"],
                          small["conv_ln_b"], l, seq=seq, d_conv=d_conv)
        q = _qproj(u_lat, small["q_norm"], pw["wq"], l, cos4, sg4, seq=seq, lat0=0,
                   q_lora=q_lora)
        kn, v, krr = _kvproj(u_lat, small["kv_norm"], pw["wkv"], l, cos4, sg4,
                             seq=seq, lat0=0, q_lora=q_lora, kv_lora=kv_lora)
        o = _attention(q, kn, krr, v, seq=seq)
        mix = _merge(hc, o, pw["wc"], pw["wo"], l, u_main, d_conv=d_conv)
        mixed = _matmul(mix, pw["w_out"], BF16, bm=1024, bn=1024, name="w_out", layer=l)
        x, h = _resnorm(x, mixed, small["post_norm_mix"], l, small["pre_norm_ffn"], l)
        act = _gateup(h, pw["wg"], pw["wu"], l)
        f = _matmul(act, pw["w_dn"], BF16, bm=512, bn=512, name="w_down", layer=l)
        if l + 1 < depth:
            x, h = _resnorm(x, f, small["post_norm_ffn"], l, small["pre_norm_mix"], l + 1)
        else:
            x, _ = _resnorm(x, f, small["post_norm_ffn"], l)
    return x.reshape(b, seq, d)


def kernel(x_prompt, x_sample, pre_norm_mix, w_in, conv_w, conv_b, conv_ln_g, conv_ln_b, w_conv_out, q_norm, w_uq, kv_norm, w_ukv, w_o_attn, w_out, post_norm_mix, pre_norm_ffn, w_gate, w_up, w_down, post_norm_ffn):
    d_conv = conv_w.shape[-1]
    q_lora = q_norm.shape[-1]
    kv_lora = kv_norm.shape[-1]
    pw = _prep_weights(w_in, w_conv_out, w_uq, w_ukv, w_o_attn, w_out, w_gate, w_up, w_down,
                       d_conv=d_conv, q_lora=q_lora, kv_lora=kv_lora)
    small = dict(pre_norm_mix=pre_norm_mix, conv_w=conv_w, conv_b=conv_b, conv_ln_g=conv_ln_g,
                 conv_ln_b=conv_ln_b, q_norm=q_norm, kv_norm=kv_norm, post_norm_mix=post_norm_mix,
                 pre_norm_ffn=pre_norm_ffn, post_norm_ffn=post_norm_ffn)
    dims = (d_conv, q_lora, kv_lora)
    return (_trunk(x_prompt, pw, small, dims), _trunk(x_sample, pw, small, dims))
```
